```python
import math
import jax, jax.numpy as jnp
from jax import lax
import numpy as np

D_MODEL = 4096
BATCH = 1
SEQ = 8192
DEPTH = 4

CTX_LEN = 256
GRID_W = 64
HEAD_DIM = 128
NA_HEADS = D_MODEL // (2 * HEAD_DIM)
NA_WIDTH = NA_HEADS * HEAD_DIM
NA_WIN_ROWS = 8
NA_WIN_COLS = 16
CONV_CH = D_MODEL - NA_WIDTH
CONV_WIDTH = 31
NA_IN = 3 * NA_WIDTH + 2 * CONV_CH
DIFF_HEADS = D_MODEL // (2 * HEAD_DIM)
DIFF_WIDTH = DIFF_HEADS * 2 * HEAD_DIM
D_FF = 5632
FFN_CONV_WIDTH = 3
ROPE_BASE = 10000.0
Q_BLOCK = 128
EPS = 1e-6
N_EVEN = (DEPTH + 1) // 2
N_ODD = DEPTH // 2

kernel_name = 'hybrid_natten_conformer_diffattn_dit'


def rms_norm(x, w):
    x32 = x.astype(jnp.float32)
    y = x32 * lax.rsqrt(jnp.mean(x32 * x32, axis=-1, keepdims=True) + EPS)
    return (y * w.astype(jnp.float32)).astype(x.dtype)


def layer_norm(x, w, b):
    x32 = x.astype(jnp.float32)
    mu = jnp.mean(x32, axis=-1, keepdims=True)
    xc = x32 - mu
    y = xc * lax.rsqrt(jnp.mean(xc * xc, axis=-1, keepdims=True) + EPS)
    return (y * w.astype(jnp.float32) + b.astype(jnp.float32)).astype(x.dtype)


def modulate(h, shift, scale):
    return h * (1 + scale) + shift


def dwconv(x, w, b):
    k = w.shape[0]
    y = lax.conv_general_dilated(x, w[:, None, :], window_strides=(1,),
                                 padding=[(k // 2, k - 1 - k // 2)],
                                 dimension_numbers=('NWC', 'WIO', 'NWC'),
                                 feature_group_count=x.shape[-1])
    return y + b


def axial_rope_tables(n):
    t = jnp.arange(n)
    row = (t // GRID_W).astype(jnp.float32)
    col = (t % GRID_W).astype(jnp.float32)
    per_axis = HEAD_DIM // 2
    inv = ROPE_BASE ** (-jnp.arange(0, per_axis, 2, dtype=jnp.float32) / per_axis)
    ang = jnp.stack([row[:, None] * inv, col[:, None] * inv], axis=1)
    return jnp.cos(ang), jnp.sin(ang)


def apply_axial_rope(x, cos, sin):
    b, n, h, dh = x.shape
    xr = x.astype(jnp.float32).reshape(b, n, h, 2, 2, dh // 4)
    x1, x2 = xr[..., 0, :], xr[..., 1, :]
    cs, sn = cos[None, :, None], sin[None, :, None]
    out = jnp.stack([x1 * cs - x2 * sn, x2 * cs + x1 * sn], axis=-2)
    return out.reshape(b, n, h, dh).astype(x.dtype)


def dense_attention(q, k, v):
    s = jnp.einsum('bqhd,bkhd->bhqk', q, k, preferred_element_type=jnp.float32) * (q.shape[-1] ** -0.5)
    p = jax.nn.softmax(s, axis=-1).astype(v.dtype)
    return jnp.einsum('bhqk,bkhd->bqhd', p, v)


def neighborhood_attention(q, k, v, k_ctx, v_ctx, rpb):
    b, n, h, dh = q.shape
    rows = n // GRID_W
    kr = min(NA_WIN_ROWS, rows)
    kc = NA_WIN_COLS
    scale = dh ** -0.5
    qg = q.reshape(b, rows, GRID_W, h, dh)
    kg = k.reshape(b, rows, GRID_W, h, dh)
    vg = v.reshape(b, rows, GRID_W, h, dh)
    cols = jnp.arange(GRID_W)
    col_start = jnp.clip(cols - kc // 2, 0, GRID_W - kc)
    col_idx = col_start[:, None] + jnp.arange(kc)[None, :]
    col_off = col_idx - cols[:, None] + (NA_WIN_COLS - 1)

    def one_row(r):
        rs = jnp.clip(r - kr // 2, 0, rows - kr)
        q_r = lax.dynamic_index_in_dim(qg, r, axis=1, keepdims=False)
        k_r = lax.dynamic_slice_in_dim(kg, rs, kr, axis=1)[:, :, col_idx]
        v_r = lax.dynamic_slice_in_dim(vg, rs, kr, axis=1)[:, :, col_idx]
        row_off = rs + jnp.arange(kr) - r + (NA_WIN_ROWS - 1)
        bias = rpb[:, row_off[None, :, None], col_off[:, None, :]]
        s_loc = jnp.einsum('bwhd,brwkhd->bhwrk', q_r, k_r, preferred_element_type=jnp.float32) * scale + bias
        s_ctx = jnp.einsum('bwhd,bjhd->bhwj', q_r, k_ctx, preferred_element_type=jnp.float32) * scale
        s = jnp.concatenate([s_loc.reshape(b, h, GRID_W, kr * kc), s_ctx], axis=-1)
        p = jax.nn.softmax(s, axis=-1)
        p_loc = p[..., :kr * kc].reshape(b, h, GRID_W, kr, kc).astype(v.dtype)
        p_ctx = p[..., kr * kc:].astype(v.dtype)
        return (jnp.einsum('bhwrk,brwkhd->bwhd', p_loc, v_r)
                + jnp.einsum('bhwj,bjhd->bwhd', p_ctx, v_ctx))

    out = lax.map(one_row, jnp.arange(rows))
    return jnp.transpose(out, (1, 0, 2, 3, 4)).reshape(b, n, h * dh)


def conformer_conv(a, g, conv_w, conv_b, ln_w, ln_b):
    hcv = a * jax.nn.sigmoid(g)
    hcv = dwconv(hcv, conv_w, conv_b)
    return jax.nn.silu(layer_norm(hcv, ln_w, ln_b))


def na_conv_mixer(h_ctx, h_lat, w_in, w_out, rpb, cv_w, cv_b, cv_ln_w, cv_ln_b, ctx_out):
    b, n, _ = h_lat.shape
    lc = h_ctx.shape[1]
    heads = lambda t: t.reshape(t.shape[0], t.shape[1], NA_HEADS, HEAD_DIM)
    p_lat = h_lat @ w_in
    q_l, k_l, v_l, a_l, g_l = jnp.split(p_lat, [NA_WIDTH, 2 * NA_WIDTH, 3 * NA_WIDTH, 3 * NA_WIDTH + CONV_CH], axis=-1)
    if ctx_out:
        p_ctx = h_ctx @ w_in
        q_c, k_c, v_c, a_c, g_c = jnp.split(p_ctx, [NA_WIDTH, 2 * NA_WIDTH, 3 * NA_WIDTH, 3 * NA_WIDTH + CONV_CH], axis=-1)
    else:
        k_c, v_c = jnp.split(h_ctx @ w_in[:, NA_WIDTH:3 * NA_WIDTH], 2, axis=-1)
    k_c, v_c = heads(k_c), heads(v_c)
    o_na = neighborhood_attention(heads(q_l), heads(k_l), heads(v_l), k_c, v_c, rpb)
    o_cv = conformer_conv(a_l, g_l, cv_w, cv_b, cv_ln_w, cv_ln_b)
    y_lat = jnp.concatenate([o_na, o_cv], axis=-1) @ w_out
    y_ctx = None
    if ctx_out:
        o_na_c = dense_attention(heads(q_c), k_c, v_c).reshape(b, lc, NA_WIDTH)
        o_cv_c = conformer_conv(a_c, g_c, cv_w, cv_b, cv_ln_w, cv_ln_b)
        y_ctx = jnp.concatenate([o_na_c, o_cv_c], axis=-1) @ w_out
    return y_ctx, y_lat


def diff_attend(q1, q2, k1, k2, v, lam):
    scale = q1.shape[-1] ** -0.5
    s1 = jnp.einsum('bqhd,bkhd->bhqk', q1, k1, preferred_element_type=jnp.float32) * scale
    s2 = jnp.einsum('bqhd,bkhd->bhqk', q2, k2, preferred_element_type=jnp.float32) * scale
    a = jax.nn.softmax(s1, axis=-1) - lam * jax.nn.softmax(s2, axis=-1)
    return jnp.einsum('bhqk,bkhe->bqhe', a.astype(v.dtype), v)


def diff_mixer(h_ctx, h_lat, w_in, w_out, lam_p, subln_w, lambda_init, cos, sin, ctx_out):
    b, n, _ = h_lat.shape
    lc = h_ctx.shape[1]
    q_l, k_l, v_l = jnp.split(h_lat @ w_in, 3, axis=-1)
    q_l = apply_axial_rope(q_l.reshape(b, n, 2 * DIFF_HEADS, HEAD_DIM), cos, sin).reshape(b, n, DIFF_HEADS, 2, HEAD_DIM)
    k_l = apply_axial_rope(k_l.reshape(b, n, 2 * DIFF_HEADS, HEAD_DIM), cos, sin).reshape(b, n, DIFF_HEADS, 2, HEAD_DIM)
    v_l = v_l.reshape(b, n, DIFF_HEADS, 2 * HEAD_DIM)
    if ctx_out:
        q_c, k_c, v_c = jnp.split(h_ctx @ w_in, 3, axis=-1)
        q_c = q_c.reshape(b, lc, DIFF_HEADS, 2, HEAD_DIM)
    else:
        k_c, v_c = jnp.split(h_ctx @ w_in[:, DIFF_WIDTH:], 2, axis=-1)
    k_c = k_c.reshape(b, lc, DIFF_HEADS, 2, HEAD_DIM)
    v_c = v_c.reshape(b, lc, DIFF_HEADS, 2 * HEAD_DIM)
    lp = lam_p.astype(jnp.float32)
    lam = jnp.exp(jnp.sum(lp[0] * lp[1])) - jnp.exp(jnp.sum(lp[2] * lp[3])) + lambda_init
    keys = jnp.concatenate([k_c, k_l], axis=1)
    vals = jnp.concatenate([v_c, v_l], axis=1)
    k1, k2 = keys[..., 0, :], keys[..., 1, :]
    nb = n // Q_BLOCK
    qb = jnp.transpose(q_l.reshape(b, nb, Q_BLOCK, DIFF_HEADS, 2, HEAD_DIM), (1, 4, 0, 2, 3, 5))
    o = lax.map(lambda qq: diff_attend(qq[0], qq[1], k1, k2, vals, lam), qb)
    o = jnp.transpose(o, (1, 0, 2, 3, 4)).reshape(b, n, DIFF_HEADS, 2 * HEAD_DIM)
    o = rms_norm(o, subln_w) * (1 - lambda_init)
    y_lat = o.reshape(b, n, DIFF_WIDTH) @ w_out
    y_ctx = None
    if ctx_out:
        oc = diff_attend(q_c[..., 0, :], q_c[..., 1, :], k_c[..., 0, :], k_c[..., 1, :], v_c, lam)
        oc = rms_norm(oc, subln_w) * (1 - lambda_init)
        y_ctx = oc.reshape(b, lc, DIFF_WIDTH) @ w_out
    return y_ctx, y_lat


def conv_ffn(h, w_up, conv_w, conv_b, w_down):
    u = dwconv(h @ w_up, conv_w, conv_b)
    gate, val = jnp.split(u, 2, axis=-1)
    return (jax.nn.silu(gate) * val) @ w_down


def setup_inputs(seed: int = 0) -> dict:
    key = jax.random.key(seed)
    ks = jax.random.split(key, 26)
    f32 = jnp.float32
    D = D_MODEL

    def nrm(k, shape, s):
        return jax.random.normal(k, shape, f32) * s

    return {
        'x': nrm(ks[0], (BATCH, SEQ, D), 1.0),
        'c': nrm(ks[1], (BATCH, D), 1.0),
        'ctx': nrm(ks[2], (BATCH, CTX_LEN, D), 1.0),
        'c_ctx': nrm(ks[3], (D,), 1.0),
        'mod_w': nrm(ks[4], (DEPTH, D, 6 * D), 0.5 * D ** -0.5),
        'mod_b': nrm(ks[5], (DEPTH, 6 * D), 0.01),
        'norm1_w': 1.0 + nrm(ks[6], (DEPTH, D), 0.01),
        'norm2_w': 1.0 + nrm(ks[7], (DEPTH, D), 0.01),
        'na_w_in': nrm(ks[8], (N_EVEN, D, NA_IN), D ** -0.5),
        'na_w_out': nrm(ks[9], (N_EVEN, D, D), D ** -0.5),
        'na_rpb': nrm(ks[10], (N_EVEN, NA_HEADS, 2 * NA_WIN_ROWS - 1, 2 * NA_WIN_COLS - 1), 0.1),
        'cv_w': nrm(ks[11], (N_EVEN, CONV_WIDTH, CONV_CH), CONV_WIDTH ** -0.5),
        'cv_b': nrm(ks[12], (N_EVEN, CONV_CH), 0.01),
        'cv_ln_w': 1.0 + nrm(ks[13], (N_EVEN, CONV_CH), 0.01),
        'cv_ln_b': nrm(ks[14], (N_EVEN, CONV_CH), 0.01),
        'diff_w_in': nrm(ks[15], (N_ODD, D, 3 * DIFF_WIDTH), D ** -0.5),
        'diff_w_out': nrm(ks[16], (N_ODD, DIFF_WIDTH, D), DIFF_WIDTH ** -0.5),
        'diff_lambda': nrm(ks[17], (N_ODD, 4, HEAD_DIM), 0.1),
        'diff_subln_w': 1.0 + nrm(ks[18], (N_ODD, 2 * HEAD_DIM), 0.01),
        'ffn_w_up': nrm(ks[19], (DEPTH, D, 2 * D_FF), D ** -0.5),
        'ffn_conv_w': nrm(ks[20], (DEPTH, FFN_CONV_WIDTH, 2 * D_FF), FFN_CONV_WIDTH ** -0.5),
        'ffn_conv_b': nrm(ks[21], (DEPTH, 2 * D_FF), 0.01),
        'ffn_w_down': nrm(ks[22], (DEPTH, D_FF, D), D_FF ** -0.5),
        'final_norm_w': 1.0 + nrm(ks[23], (D,), 0.01),
    }


def reference(x, c, ctx, c_ctx, mod_w, mod_b, norm1_w, norm2_w, na_w_in, na_w_out, na_rpb,
              cv_w, cv_b, cv_ln_w, cv_ln_b, diff_w_in, diff_w_out, diff_lambda, diff_subln_w,
              ffn_w_up, ffn_conv_w, ffn_conv_b, ffn_w_down, final_norm_w):
    n_lat = x.shape[1]
    cos, sin = axial_rope_tables(n_lat)
    s_lat = jax.nn.silu(c)
    s_ctx = jax.nn.silu(c_ctx)[None, :]
    h_lat, h_ctx = x, ctx
    for l in range(DEPTH):
        last = l == DEPTH - 1
        i = l // 2
        m_lat = jnp.split((s_lat @ mod_w[l] + mod_b[l])[:, None, :], 6, axis=-1)
        m_ctx = jnp.split((s_ctx @ mod_w[l] + mod_b[l])[:, None, :], 6, axis=-1)
        a_lat = modulate(rms_norm(h_lat, norm1_w[l]), m_lat[0], m_lat[1])
        a_ctx = modulate(rms_norm(h_ctx, norm1_w[l]), m_ctx[0], m_ctx[1])
        if l % 2 == 0:
            y_ctx, y_lat = na_conv_mixer(a_ctx, a_lat, na_w_in[i], na_w_out[i], na_rpb[i], cv_w[i], cv_b[i],
                                         cv_ln_w[i], cv_ln_b[i], not last)
        else:
            lambda_init = 0.8 - 0.6 * math.exp(-0.3 * l)
            y_ctx, y_lat = diff_mixer(a_ctx, a_lat, diff_w_in[i], diff_w_out[i], diff_lambda[i], diff_subln_w[i],
                                      lambda_init, cos, sin, not last)
        h_lat = h_lat + m_lat[2] * y_lat
        f_lat = modulate(rms_norm(h_lat, norm2_w[l]), m_lat[3], m_lat[4])
        h_lat = h_lat + m_lat[5] * conv_ffn(f_lat, ffn_w_up[l], ffn_conv_w[l], ffn_conv_b[l], ffn_w_down[l])
        if not last:
            h_ctx = h_ctx + m_ctx[2] * y_ctx
            f_ctx = modulate(rms_norm(h_ctx, norm2_w[l]), m_ctx[3], m_ctx[4])
            h_ctx = h_ctx + m_ctx[5] * conv_ffn(f_ctx, ffn_w_up[l], ffn_conv_w[l], ffn_conv_b[l], ffn_w_down[l])
    return rms_norm(h_lat, final_norm_w)
```

```python
import functools
import math

import jax
import jax.numpy as jnp
import numpy as np
from jax import lax
from jax.experimental import pallas as pl
from jax.experimental.pallas import tpu as pltpu

GRID_W = 64
HEAD_DIM = 128
NA_WIN_ROWS = 8
NA_WIN_COLS = 16
CONV_WIDTH = 31
FFN_CONV_WIDTH = 3
ROPE_BASE = 10000.0
EPS = 1e-6

V7X_VMEM_BYTES = 64 * 1024 * 1024
VMEM_LIMIT = V7X_VMEM_BYTES * 7 // 8
LANES = 128
SUBLANES = 8
NEG_INF = -1e30

BF16 = jnp.bfloat16
F32 = jnp.float32


def _cparams(n_grid_axes):
    return pltpu.CompilerParams(dimension_semantics=("arbitrary",) * n_grid_axes,
                                vmem_limit_bytes=VMEM_LIMIT)


def _dot(a, b):
    return jnp.dot(a, b, preferred_element_type=F32)


def _dot_nt(a, b):
    return lax.dot_general(a, b, (((1,), (1,)), ((), ())), preferred_element_type=F32)


def _seg_select(row0, n_rows, n_lat, two_rows):
    rid = row0 + lax.broadcasted_iota(jnp.int32, (n_rows, 1), 0)
    return jnp.where(rid >= n_lat, two_rows[1:2, :], two_rows[0:1, :])


def _mod_kernel(c_ref, w_ref, b_ref, o_ref, s_ref, *, tn):
    k_dim = w_ref.shape[0]

    @pl.when((pl.program_id(0) == 0) & (pl.program_id(1) == 0))
    def _():
        x = c_ref[...]
        s = x * jax.nn.sigmoid(x)
        s_ref[0] = jnp.broadcast_to(s[:, 0:1], (k_dim, LANES))
        s_ref[1] = jnp.broadcast_to(s[:, 1:2], (k_dim, LANES))

    n_groups = tn // LANES

    def body(k, accs):
        r = pl.multiple_of(k * SUBLANES, SUBLANES)
        s0 = s_ref[0, pl.ds(r, SUBLANES), :]
        s1 = s_ref[1, pl.ds(r, SUBLANES), :]
        out = []
        for g in range(n_groups):
            w = w_ref[pl.ds(r, SUBLANES), g * LANES:(g + 1) * LANES]
            out.append(accs[2 * g] + s0 * w)
            out.append(accs[2 * g + 1] + s1 * w)
        return tuple(out)

    zeros = tuple(jnp.zeros((SUBLANES, LANES), F32) for _ in range(2 * n_groups))
    accs = lax.fori_loop(0, k_dim // SUBLANES, body, zeros, unroll=8)
    for g in range(n_groups):
        b = b_ref[:, g * LANES:(g + 1) * LANES]
        o_ref[0:1, g * LANES:(g + 1) * LANES] = jnp.sum(accs[2 * g], axis=0, keepdims=True) + b
        o_ref[1:2, g * LANES:(g + 1) * LANES] = jnp.sum(accs[2 * g + 1], axis=0, keepdims=True) + b


def adaln_modulation(c2, mod_w, mod_b, *, tn=512):
    n_layers, k_dim, n = mod_w.shape
    tn = min(tn, n)
    return pl.pallas_call(
        functools.partial(_mod_kernel, tn=tn),
        out_shape=jax.ShapeDtypeStruct((n_layers, 2, n), F32),
        grid=(n_layers, n // tn),
        in_specs=[
            pl.BlockSpec((k_dim, 2), lambda l, j: (0, 0)),
            pl.BlockSpec((None, k_dim, tn), lambda l, j: (l, 0, j)),
            pl.BlockSpec((None, 1, tn), lambda l, j: (l, 0, j)),
        ],
        out_specs=pl.BlockSpec((None, 2, tn), lambda l, j: (l, 0, j)),
        scratch_shapes=[pltpu.VMEM((2, k_dim, LANES), F32)],
        compiler_params=_cparams(2),
        name="adaln_mod",
    )(c2, mod_w, mod_b.reshape(n_layers, 1, n))


def _norm_mod_kernel(h_ref, w_ref, shift_ref, scale_ref, o_ref, *, n_lat, tm):
    x = h_ref[...]
    y = x * lax.rsqrt(jnp.mean(x * x, axis=-1, keepdims=True) + EPS)
    y = y * w_ref[...]
    row0 = pl.program_id(0) * tm
    scale = _seg_select(row0, tm, n_lat, scale_ref[...])
    shift = _seg_select(row0, tm, n_lat, shift_ref[...])
    o_ref[...] = (y * (1.0 + scale) + shift).astype(o_ref.dtype)


def norm_modulate(h, w, shift2, scale2, *, n_lat, out_dtype, n_rows=None, tm=256):
    m, d = h.shape
    m = m if n_rows is None else n_rows
    tm = min(tm, m)
    return pl.pallas_call(
        functools.partial(_norm_mod_kernel, n_lat=n_lat, tm=tm),
        out_shape=jax.ShapeDtypeStruct((m, d), out_dtype),
        grid=(m // tm,),
        in_specs=[
            pl.BlockSpec((tm, d), lambda i: (i, 0)),
            pl.BlockSpec((1, d), lambda i: (0, 0)),
            pl.BlockSpec((2, d), lambda i: (0, 0)),
            pl.BlockSpec((2, d), lambda i: (0, 0)),
        ],
        out_specs=pl.BlockSpec((tm, d), lambda i: (i, 0)),
        compiler_params=_cparams(1),
        name="norm_modulate",
    )(h, w.reshape(1, d), shift2, scale2)


def _mm_kernel(a_ref, w_ref, o_ref):
    o_ref[...] = _dot(a_ref[...], w_ref[...]).astype(o_ref.dtype)


def matmul(a, w, *, out_dtype, tm, tn):
    m, k = a.shape
    n = w.shape[1]
    return pl.pallas_call(
        _mm_kernel,
        out_shape=jax.ShapeDtypeStruct((m, n), out_dtype),
        grid=(m // tm, n // tn),
        in_specs=[pl.BlockSpec((tm, k), lambda i, j: (i, 0)),
                  pl.BlockSpec((k, tn), lambda i, j: (0, j))],
        out_specs=pl.BlockSpec((tm, tn), lambda i, j: (i, j)),
        compiler_params=_cparams(2),
        name="matmul",
    )(a, w)


def _mm_rope_kernel(a_ref, w_ref, cos_ref, sa_ref, sb_ref, o_ref, *, rope_tiles, tn):
    acc = _dot(a_ref[...], w_ref[...])
    j = pl.program_id(1)

    @pl.when(j < rope_tiles)
    def _():
        reps = tn // HEAD_DIM
        cos = jnp.concatenate([cos_ref[...]] * reps, axis=1)
        sa = jnp.concatenate([sa_ref[...]] * reps, axis=1)
        sb = jnp.concatenate([sb_ref[...]] * reps, axis=1)
        up = pltpu.roll(acc, tn - HEAD_DIM // 4, 1)
        dn = pltpu.roll(acc, HEAD_DIM // 4, 1)
        o_ref[...] = (acc * cos + up * sa + dn * sb).astype(o_ref.dtype)

    @pl.when(j >= rope_tiles)
    def _():
        o_ref[...] = acc.astype(o_ref.dtype)


def matmul_rope(a, w, cos, sin_a, sin_b, *, rope_cols, out_dtype, tm, tn):
    m, k = a.shape
    n = w.shape[1]
    return pl.pallas_call(
        functools.partial(_mm_rope_kernel, rope_tiles=rope_cols // tn, tn=tn),
        out_shape=jax.ShapeDtypeStruct((m, n), out_dtype),
        grid=(m // tm, n // tn),
        in_specs=[pl.BlockSpec((tm, k), lambda i, j: (i, 0)),
                  pl.BlockSpec((k, tn), lambda i, j: (0, j)),
                  pl.BlockSpec((tm, HEAD_DIM), lambda i, j: (i, 0)),
                  pl.BlockSpec((tm, HEAD_DIM), lambda i, j: (i, 0)),
                  pl.BlockSpec((tm, HEAD_DIM), lambda i, j: (i, 0))],
        out_specs=pl.BlockSpec((tm, tn), lambda i, j: (i, j)),
        compiler_params=_cparams(2),
        name="matmul_rope",
    )(a, w, cos, sin_a, sin_b)


def _mm_res_kernel(*refs, n_a, n_lat, tm):
    a_refs = refs[:n_a]
    w_ref, res_ref, gate_ref, o_ref = refs[n_a:]
    acc = None
    k0 = 0
    for a_ref in a_refs:
        kk = a_ref.shape[1]
        part = _dot(a_ref[...], w_ref[k0:k0 + kk, :])
        acc = part if acc is None else acc + part
        k0 += kk
    gate = _seg_select(pl.program_id(0) * tm, tm, n_lat, gate_ref[...])
    o_ref[...] = res_ref[...] + gate * acc


def matmul_residual(a_list, w, res, gate2, *, n_lat, tm, tn):
    m = res.shape[0]
    k, n = w.shape
    in_specs = [pl.BlockSpec((tm, a.shape[1]), lambda i, j: (i, 0)) for a in a_list]
    in_specs += [pl.BlockSpec((k, tn), lambda i, j: (0, j)),
                 pl.BlockSpec((tm, tn), lambda i, j: (i, j)),
                 pl.BlockSpec((2, tn), lambda i, j: (0, j))]
    return pl.pallas_call(
        functools.partial(_mm_res_kernel, n_a=len(a_list), n_lat=n_lat, tm=tm),
        out_shape=jax.ShapeDtypeStruct((m, n), F32),
        grid=(m // tm, n // tn),
        in_specs=in_specs,
        out_specs=pl.BlockSpec((tm, tn), lambda i, j: (i, j)),
        compiler_params=_cparams(2),
        name="matmul_residual",
    )(*a_list, w, res, gate2)


def _na_window_start(t, rows_per_tile, win_rows, n_rows):
    return jnp.clip(t * rows_per_tile - NA_WIN_ROWS // 2, 0, n_rows - win_rows)


def _na_kernel(pat_ref, q_ref, k_ref, v_ref, kc_ref, vc_ref, bias_ref, o_ref, *,
               rows_per_tile, win_rows, n_rows):
    del pat_ref
    t = pl.program_id(1)
    scale = HEAD_DIM ** -0.5
    start = pl.multiple_of(_na_window_start(t, rows_per_tile, win_rows, n_rows) * GRID_W, GRID_W)
    q = q_ref[...]
    k_loc = k_ref[pl.ds(start, win_rows * GRID_W), :]
    v_loc = v_ref[pl.ds(start, win_rows * GRID_W), :]
    s_loc = _dot_nt(q, k_loc) * scale + bias_ref[...]
    s_ctx = _dot_nt(q, kc_ref[...]) * scale
    m = jnp.maximum(jnp.max(s_loc, axis=-1, keepdims=True), jnp.max(s_ctx, axis=-1, keepdims=True))
    p_loc = jnp.exp(s_loc - m)
    p_ctx = jnp.exp(s_ctx - m)
    denom = jnp.sum(p_loc, axis=-1, keepdims=True) + jnp.sum(p_ctx, axis=-1, keepdims=True)
    o = _dot(p_loc.astype(BF16), v_loc) + _dot(p_ctx.astype(BF16), vc_ref[...])
    o_ref[...] = (o / denom).astype(o_ref.dtype)


def _na_tile_patterns(n_rows, rows_per_tile):
    win_rows = rows_per_tile + NA_WIN_ROWS - 1
    keys, pat_of_tile, rep_tile = {}, [], []
    for t in range(n_rows // rows_per_tile):
        r0 = t * rows_per_tile
        us = min(max(r0 - NA_WIN_ROWS // 2, 0), n_rows - win_rows)
        rs = tuple(min(max(r0 + i - NA_WIN_ROWS // 2, 0), n_rows - NA_WIN_ROWS) - us
                   for i in range(rows_per_tile))
        key = (r0 - us, rs)
        if key not in keys:
            keys[key] = len(keys)
            rep_tile.append(t)
        pat_of_tile.append(keys[key])
    return np.asarray(pat_of_tile, np.int32), rep_tile, win_rows


def _na_bias_table(rpb, n_rows, rows_per_tile):
    pat_of_tile, rep_tile, win_rows = _na_tile_patterns(n_rows, rows_per_tile)
    heads = rpb.shape[0]
    w = np.arange(GRID_W)
    col_start = np.clip(w - NA_WIN_COLS // 2, 0, GRID_W - NA_WIN_COLS)
    wk = np.arange(GRID_W)
    col_valid = (wk[None, :] >= col_start[:, None]) & (wk[None, :] < col_start[:, None] + NA_WIN_COLS)
    col_off = np.clip(wk[None, :] - w[:, None] + NA_WIN_COLS - 1, 0, 2 * NA_WIN_COLS - 2)
    tables = []
    for t in rep_tile:
        r0 = t * rows_per_tile
        us = min(max(r0 - NA_WIN_ROWS // 2, 0), n_rows - win_rows)
        r = r0 + np.arange(rows_per_tile)
        rs = np.clip(r - NA_WIN_ROWS // 2, 0, n_rows - NA_WIN_ROWS)
        key_row = us + np.arange(win_rows)
        row_valid = (key_row[None, :] >= rs[:, None]) & (key_row[None, :] < rs[:, None] + NA_WIN_ROWS)
        row_off = np.clip(key_row[None, :] - r[:, None] + NA_WIN_ROWS - 1, 0, 2 * NA_WIN_ROWS - 2)
        b = rpb[:, row_off[:, :, None, None], col_off[None, None, :, :]]
        valid = row_valid[:, :, None, None] & col_valid[None, None, :, :]
        b = jnp.where(valid[None], b, NEG_INF)
        b = jnp.transpose(b, (0, 1, 3, 2, 4)).reshape(heads, rows_per_tile * GRID_W, win_rows * GRID_W)
        tables.append(b)
    return jnp.stack(tables, axis=0).astype(F32), pat_of_tile, win_rows


def neighborhood_attention(qkv, rpb, *, n_lat, n_ctx, rows_per_tile=4):
    heads = rpb.shape[0]
    n_rows = n_lat // GRID_W
    bias, pat_of_tile, win_rows = _na_bias_table(rpb, n_rows, rows_per_tile)
    tq = rows_per_tile * GRID_W
    ctx_blk = n_lat // n_ctx
    grid_spec = pltpu.PrefetchScalarGridSpec(
        num_scalar_prefetch=1,
        grid=(heads, n_rows // rows_per_tile),
        in_specs=[
            pl.BlockSpec((tq, HEAD_DIM), lambda h, t, pat: (t, h)),
            pl.BlockSpec((n_lat, HEAD_DIM), lambda h, t, pat: (0, heads + h)),
            pl.BlockSpec((n_lat, HEAD_DIM), lambda h, t, pat: (0, 2 * heads + h)),
            pl.BlockSpec((n_ctx, HEAD_DIM), lambda h, t, pat: (ctx_blk, heads + h)),
            pl.BlockSpec((n_ctx, HEAD_DIM), lambda h, t, pat: (ctx_blk, 2 * heads + h)),
            pl.BlockSpec((None, None, tq, win_rows * GRID_W), lambda h, t, pat: (pat[t], h, 0, 0)),
        ],
        out_specs=pl.BlockSpec((tq, HEAD_DIM), lambda h, t, pat: (t, h)),
    )
    return pl.pallas_call(
        functools.partial(_na_kernel, rows_per_tile=rows_per_tile, win_rows=win_rows, n_rows=n_rows),
        out_shape=jax.ShapeDtypeStruct((n_lat, heads * HEAD_DIM), BF16),
        grid_spec=grid_spec,
        compiler_params=_cparams(2),
        name="neighborhood_attention",
    )(jnp.asarray(pat_of_tile), qkv, qkv, qkv, qkv, qkv, bias)


def _ctx_attn_kernel(q_ref, k_ref, v_ref, o_ref):
    s = _dot_nt(q_ref[...], k_ref[...]) * (HEAD_DIM ** -0.5)
    p = jnp.exp(s - jnp.max(s, axis=-1, keepdims=True))
    denom = jnp.sum(p, axis=-1, keepdims=True)
    o_ref[...] = (_dot(p.astype(BF16), v_ref[...]) / denom).astype(o_ref.dtype)


def context_attention(qkv, *, heads, n_lat, n_ctx):
    ctx_blk = n_lat // n_ctx
    return pl.pallas_call(
        _ctx_attn_kernel,
        out_shape=jax.ShapeDtypeStruct((n_ctx, heads * HEAD_DIM), BF16),
        grid=(heads,),
        in_specs=[pl.BlockSpec((n_ctx, HEAD_DIM), lambda h: (ctx_blk, h)),
                  pl.BlockSpec((n_ctx, HEAD_DIM), lambda h: (ctx_blk, heads + h)),
                  pl.BlockSpec((n_ctx, HEAD_DIM), lambda h: (ctx_blk, 2 * heads + h))],
        out_specs=pl.BlockSpec((n_ctx, HEAD_DIM), lambda h: (0, h)),
        compiler_params=_cparams(1),
        name="context_attention",
    )(qkv, qkv, qkv)


CONV_HALO = 16


def _conformer_kernel(ap_ref, gp_ref, a_ref, g_ref, an_ref, gn_ref, cw_ref, cb_ref, lw_ref, lb_ref, o_ref,
                      buf_ref, y_ref, *, tm, first_tiles, last_tiles):
    i = pl.program_id(0)
    ch = a_ref.shape[1]
    has_prev = jnp.logical_not(functools.reduce(jnp.logical_or, [i == t for t in first_tiles]))
    has_next = jnp.logical_not(functools.reduce(jnp.logical_or, [i == t for t in last_tiles]))

    def glu(a, g):
        return a * jax.nn.sigmoid(g)

    buf_ref[0:CONV_HALO, :] = jnp.where(has_prev, glu(ap_ref[...], gp_ref[...]), 0.0)
    buf_ref[CONV_HALO:CONV_HALO + tm, :] = glu(a_ref[...], g_ref[...])
    buf_ref[CONV_HALO + tm:, :] = jnp.where(has_next, glu(an_ref[...], gn_ref[...]), 0.0)

    base = CONV_HALO - CONV_WIDTH // 2

    def chunk(c, carry):
        col = pl.multiple_of(c * LANES, LANES)
        acc = jnp.zeros((tm, LANES), F32) + cb_ref[:, pl.ds(col, LANES)]
        for tap in range(CONV_WIDTH):
            acc = acc + buf_ref[base + tap:base + tap + tm, pl.ds(col, LANES)] * cw_ref[tap:tap + 1, pl.ds(col, LANES)]
        y_ref[:, pl.ds(col, LANES)] = acc
        return carry

    lax.fori_loop(0, ch // LANES, chunk, 0)

    y = y_ref[...]
    mu = jnp.mean(y, axis=-1, keepdims=True)
    yc = y - mu
    z = yc * lax.rsqrt(jnp.mean(yc * yc, axis=-1, keepdims=True) + EPS)
    z = z * lw_ref[...] + lb_ref[...]
    o_ref[...] = (z * jax.nn.sigmoid(z)).astype(o_ref.dtype)


def conformer_conv(ag, cv_w, cv_b, ln_w, ln_b, *, n_lat, tm=256):
    m, two_c = ag.shape
    ch = two_c // 2
    n_tiles = m // tm
    hb = tm // CONV_HALO
    n_hblk = m // CONV_HALO
    first_tiles = (0, n_lat // tm)
    last_tiles = (n_lat // tm - 1, n_tiles - 1)
    prev_map = lambda c: (lambda i: (jnp.maximum(i * hb - 1, 0), c))
    next_map = lambda c: (lambda i: (jnp.minimum((i + 1) * hb, n_hblk - 1), c))
    cur_map = lambda c: (lambda i: (i, c))
    vec = lambda: pl.BlockSpec((1, ch), lambda i: (0, 0))
    return pl.pallas_call(
        functools.partial(_conformer_kernel, tm=tm, first_tiles=first_tiles, last_tiles=last_tiles),
        out_shape=jax.ShapeDtypeStruct((m, ch), BF16),
        grid=(n_tiles,),
        in_specs=[pl.BlockSpec((CONV_HALO, ch), prev_map(0)), pl.BlockSpec((CONV_HALO, ch), prev_map(1)),
                  pl.BlockSpec((tm, ch), cur_map(0)), pl.BlockSpec((tm, ch), cur_map(1)),
                  pl.BlockSpec((CONV_HALO, ch), next_map(0)), pl.BlockSpec((CONV_HALO, ch), next_map(1)),
                  pl.BlockSpec((CONV_WIDTH, ch), lambda i: (0, 0)), vec(), vec(), vec()],
        out_specs=pl.BlockSpec((tm, ch), lambda i: (i, 0)),
        scratch_shapes=[pltpu.VMEM((tm + 2 * CONV_HALO, ch), F32), pltpu.VMEM((tm, ch), F32)],
        compiler_params=_cparams(1),
        name="conformer_conv",
    )(ag, ag, ag, ag, ag, ag, cv_w, cv_b.reshape(1, ch), ln_w.reshape(1, ch), ln_b.reshape(1, ch))


DIFF_Q_TILE = 256
DIFF_KV_CHUNK = 512


def _diff_attn_kernel(lam_ref, q_ref, k_ref, v_ref, sw_ref, o_ref,
                      m1_ref, l1_ref, acc1_ref, m2_ref, l2_ref, acc2_ref, *,
                      segments, lambda_init):
    scale = HEAD_DIM ** -0.5
    q1 = q_ref[:, :HEAD_DIM]
    q2 = q_ref[:, HEAD_DIM:]
    for m_ref, l_ref, acc_ref in ((m1_ref, l1_ref, acc1_ref), (m2_ref, l2_ref, acc2_ref)):
        m_ref[...] = jnp.full(m_ref.shape, NEG_INF, F32)
        l_ref[...] = jnp.zeros(l_ref.shape, F32)
        acc_ref[...] = jnp.zeros(acc_ref.shape, F32)

    def online(q, kc, vc, m_ref, l_ref, acc_ref):
        s = _dot_nt(q, kc) * scale
        m_old = m_ref[...]
        m_new = jnp.maximum(m_old, jnp.max(s, axis=-1, keepdims=True))
        alpha = jnp.exp(m_old - m_new)
        p = jnp.exp(s - m_new)
        l_ref[...] = alpha * l_ref[...] + jnp.sum(p, axis=-1, keepdims=True)
        acc_ref[...] = alpha * acc_ref[...] + _dot(p.astype(BF16), vc)
        m_ref[...] = m_new

    def visit(row, tk):
        kc = k_ref[pl.ds(row, tk), :]
        vc = v_ref[pl.ds(row, tk), :]
        online(q1, kc[:, :HEAD_DIM], vc, m1_ref, l1_ref, acc1_ref)
        online(q2, kc[:, HEAD_DIM:], vc, m2_ref, l2_ref, acc2_ref)

    for first_row, n_chunks, tk in segments:
        if n_chunks == 1:
            visit(first_row, tk)
        else:
            def body(c, carry, first_row=first_row, tk=tk):
                visit(pl.multiple_of(first_row + c * tk, tk), tk)
                return carry
            lax.fori_loop(0, n_chunks, body, 0)

    lp = lam_ref[...]
    lam = (jnp.exp(jnp.sum(lp[0:1] * lp[1:2], axis=-1, keepdims=True))
           - jnp.exp(jnp.sum(lp[2:3] * lp[3:4], axis=-1, keepdims=True)) + lambda_init)
    o = acc1_ref[...] / l1_ref[...] - lam * (acc2_ref[...] / l2_ref[...])
    o = o * lax.rsqrt(jnp.mean(o * o, axis=-1, keepdims=True) + EPS)
    o_ref[...] = ((o * sw_ref[...]) * (1.0 - lambda_init)).astype(o_ref.dtype)


def diff_attention(qkv, lam_p, subln_w, *, heads, q_row0, n_q, segments, lambda_init, tq):
    m = qkv.shape[0]
    hd2 = 2 * HEAD_DIM
    q_blk0 = q_row0 // tq
    scratch = []
    for _ in range(2):
        scratch += [pltpu.VMEM((tq, 1), F32), pltpu.VMEM((tq, 1), F32), pltpu.VMEM((tq, hd2), F32)]
    return pl.pallas_call(
        functools.partial(_diff_attn_kernel, segments=segments, lambda_init=lambda_init),
        out_shape=jax.ShapeDtypeStruct((n_q, heads * hd2), BF16),
        grid=(heads, n_q // tq),
        in_specs=[pl.BlockSpec((4, HEAD_DIM), lambda h, i: (0, 0)),
                  pl.BlockSpec((tq, hd2), lambda h, i: (q_blk0 + i, h)),
                  pl.BlockSpec((m, hd2), lambda h, i: (0, heads + h)),
                  pl.BlockSpec((m, hd2), lambda h, i: (0, 2 * heads + h)),
                  pl.BlockSpec((1, hd2), lambda h, i: (0, 0))],
        out_specs=pl.BlockSpec((tq, hd2), lambda h, i: (i, h)),
        scratch_shapes=scratch,
        compiler_params=_cparams(2),
        name="diff_attention",
    )(lam_p.astype(F32), qkv, qkv, qkv, subln_w.reshape(1, hd2))


FFN_HALO = 8


def _ffn_gate_kernel(gp_ref, g_ref, gn_ref, vp_ref, v_ref, vn_ref, wg_ref, bg_ref, wv_ref, bv_ref, o_ref,
                     gbuf_ref, vbuf_ref, *, tm, first_tiles, last_tiles):
    i = pl.program_id(0)
    has_prev = jnp.logical_not(functools.reduce(jnp.logical_or, [i == t for t in first_tiles]))
    has_next = jnp.logical_not(functools.reduce(jnp.logical_or, [i == t for t in last_tiles]))

    def conv(p_ref, c_ref, n_ref, buf_ref, w_ref, b_ref):
        buf_ref[0:FFN_HALO, :] = jnp.where(has_prev, p_ref[...], 0.0)
        buf_ref[FFN_HALO:FFN_HALO + tm, :] = c_ref[...]
        buf_ref[FFN_HALO + tm:, :] = jnp.where(has_next, n_ref[...], 0.0)
        base = FFN_HALO - FFN_CONV_WIDTH // 2
        acc = b_ref[...]
        for tap in range(FFN_CONV_WIDTH):
            acc = acc + buf_ref[base + tap:base + tap + tm, :] * w_ref[tap:tap + 1, :]
        return acc

    gate = conv(gp_ref, g_ref, gn_ref, gbuf_ref, wg_ref, bg_ref)
    val = conv(vp_ref, v_ref, vn_ref, vbuf_ref, wv_ref, bv_ref)
    o_ref[...] = (gate * jax.nn.sigmoid(gate) * val).astype(o_ref.dtype)


def ffn_conv_gate(u, conv_w, conv_b, *, n_lat, tm=256, tn=512):
    m, two_f = u.shape
    f = two_f // 2
    n_tiles = m // tm
    hb = tm // FFN_HALO
    n_hblk = m // FFN_HALO
    voff = f // tn
    first_tiles = (0, n_lat // tm)
    last_tiles = (n_lat // tm - 1, n_tiles - 1)
    prev_map = lambda off: (lambda i, j: (jnp.maximum(i * hb - 1, 0), j + off))
    next_map = lambda off: (lambda i, j: (jnp.minimum((i + 1) * hb, n_hblk - 1), j + off))
    cur_map = lambda off: (lambda i, j: (i, j + off))
    halo = lambda fn: pl.BlockSpec((FFN_HALO, tn), fn)
    b2 = conv_b.reshape(1, two_f)
    return pl.pallas_call(
        functools.partial(_ffn_gate_kernel, tm=tm, first_tiles=first_tiles, last_tiles=last_tiles),
        out_shape=jax.ShapeDtypeStruct((m, f), BF16),
        grid=(n_tiles, f // tn),
        in_specs=[halo(prev_map(0)), pl.BlockSpec((tm, tn), cur_map(0)), halo(next_map(0)),
                  halo(prev_map(voff)), pl.BlockSpec((tm, tn), cur_map(voff)), halo(next_map(voff)),
                  pl.BlockSpec((FFN_CONV_WIDTH, tn), lambda i, j: (0, j)),
                  pl.BlockSpec((1, tn), lambda i, j: (0, j)),
                  pl.BlockSpec((FFN_CONV_WIDTH, tn), lambda i, j: (0, j + voff)),
                  pl.BlockSpec((1, tn), lambda i, j: (0, j + voff))],
        out_specs=pl.BlockSpec((tm, tn), lambda i, j: (i, j)),
        scratch_shapes=[pltpu.VMEM((tm + 2 * FFN_HALO, tn), F32), pltpu.VMEM((tm + 2 * FFN_HALO, tn), F32)],
        compiler_params=_cparams(2),
        name="ffn_conv_gate",
    )(u, u, u, u, u, u, conv_w, b2, conv_w, b2)


def _rope_tables(n_lat, n_ctx):
    t = jnp.arange(n_lat)
    row = (t // GRID_W).astype(F32)
    col = (t % GRID_W).astype(F32)
    per_axis = HEAD_DIM // 2
    inv = ROPE_BASE ** (-jnp.arange(0, per_axis, 2, dtype=F32) / per_axis)
    ang_r, ang_c = row[:, None] * inv, col[:, None] * inv
    cr, sr, cc, sc = jnp.cos(ang_r), jnp.sin(ang_r), jnp.cos(ang_c), jnp.sin(ang_c)
    zero = jnp.zeros_like(sr)
    cos = jnp.concatenate([cr, cr, cc, cc], axis=1)
    sin_a = jnp.concatenate([-sr, zero, -sc, zero], axis=1)
    sin_b = jnp.concatenate([zero, sr, zero, sc], axis=1)
    pad = lambda x, v: jnp.concatenate([x, jnp.full((n_ctx, HEAD_DIM), v, F32)], axis=0)
    return pad(cos, 1.0), pad(sin_a, 0.0), pad(sin_b, 0.0)


def kernel(x, c, ctx, c_ctx, mod_w, mod_b, norm1_w, norm2_w, na_w_in, na_w_out, na_rpb, cv_w, cv_b, cv_ln_w,
           cv_ln_b, diff_w_in, diff_w_out, diff_lambda, diff_subln_w, ffn_w_up, ffn_conv_w, ffn_conv_b,
           ffn_w_down, final_norm_w):
    assert x.shape[0] == 1 and ctx.shape[0] == 1
    n_lat, d = x.shape[1], x.shape[2]
    n_ctx = ctx.shape[1]
    depth = mod_w.shape[0]
    m = n_lat + n_ctx
    na_heads = na_rpb.shape[1]
    na_width = na_heads * HEAD_DIM
    diff_heads = diff_w_in.shape[2] // (3 * 2 * HEAD_DIM)
    diff_width = diff_heads * 2 * HEAD_DIM
    tm = m // 11 if m % 11 == 0 and (m // 11) % 256 == 0 else 256
    tn = 512

    h = jnp.concatenate([x[0], ctx[0]], axis=0)
    mod = adaln_modulation(jnp.stack([c[0], c_ctx], axis=1), mod_w, mod_b)
    mod = mod.reshape(depth, 2, 6, d)
    cos, sin_a, sin_b = _rope_tables(n_lat, n_ctx)

    for l in range(depth):
        i = l // 2
        shift1, scale1, gate1, shift2, scale2, gate2 = (mod[l, :, j, :] for j in range(6))
        a = norm_modulate(h, norm1_w[l], shift1, scale1, n_lat=n_lat, out_dtype=BF16)
        if l % 2 == 0:
            w_in = na_w_in[i].astype(BF16)
            qkv = matmul(a, w_in[:, :3 * na_width], out_dtype=BF16, tm=tm, tn=tn)
            ag = matmul(a, w_in[:, 3 * na_width:], out_dtype=F32, tm=tm, tn=tn)
            o_na_lat = neighborhood_attention(qkv, na_rpb[i], n_lat=n_lat, n_ctx=n_ctx)
            o_na_ctx = context_attention(qkv, heads=na_heads, n_lat=n_lat, n_ctx=n_ctx)
            o_na = jnp.concatenate([o_na_lat, o_na_ctx], axis=0)
            o_cv = conformer_conv(ag, cv_w[i], cv_b[i], cv_ln_w[i], cv_ln_b[i], n_lat=n_lat)
            h = matmul_residual([o_na, o_cv], na_w_out[i].astype(BF16), h, gate1, n_lat=n_lat, tm=tm, tn=tn)
        else:
            lambda_init = 0.8 - 0.6 * math.exp(-0.3 * l)
            qkv = matmul_rope(a, diff_w_in[i].astype(BF16), cos, sin_a, sin_b, rope_cols=2 * diff_width,
                              out_dtype=BF16, tm=tm, tn=tn)
            tk = min(DIFF_KV_CHUNK, n_lat)
            o_lat = diff_attention(qkv, diff_lambda[i], diff_subln_w[i], heads=diff_heads, q_row0=0, n_q=n_lat,
                                   segments=((0, n_lat // tk, tk), (n_lat, 1, n_ctx)),
                                   lambda_init=lambda_init, tq=min(DIFF_Q_TILE, n_lat))
            o_ctx = diff_attention(qkv, diff_lambda[i], diff_subln_w[i], heads=diff_heads, q_row0=n_lat, n_q=n_ctx,
                                   segments=((n_lat, 1, n_ctx),), lambda_init=lambda_init, tq=n_ctx)
            o = jnp.concatenate([o_lat, o_ctx], axis=0)
            h = matmul_residual([o], diff_w_out[i].astype(BF16), h, gate1, n_lat=n_lat, tm=tm, tn=tn)
        f = norm_modulate(h, norm2_w[l], shift2, scale2, n_lat=n_lat, out_dtype=BF16)
        u = matmul(f, ffn_w_up[l].astype(BF16), out_dtype=F32, tm=tm, tn=tn)
        act = ffn_conv_gate(u, ffn_conv_w[l], ffn_conv_b[l], n_lat=n_lat)
        h = matmul_residual([act], ffn_w_down[l].astype(BF16), h, gate2, n_lat=n_lat, tm=tm, tn=tn)

    zeros2 = jnp.zeros((2, d), F32)
    out = norm_modulate(h, final_norm_w, zeros2, zeros2, n_lat=n_lat, out_dtype=F32, n_rows=n_lat)
    return out[None]
```

```python
import functools
import math

import jax
import jax.numpy as jnp
import numpy as np
from jax import lax
from jax.experimental import pallas as pl
from jax.experimental.pallas import tpu as pltpu

GRID_W = 64
HEAD_DIM = 128
NA_WIN_ROWS = 8
NA_WIN_COLS = 16
CONV_WIDTH = 31
FFN_CONV_WIDTH = 3
ROPE_BASE = 10000.0
EPS = 1e-6

V7X_VMEM_BYTES = 64 * 1024 * 1024
VMEM_LIMIT = V7X_VMEM_BYTES * 7 // 8
LANES = 128
SUBLANES = 8
NEG_INF = -1e30

BF16 = jnp.bfloat16
F32 = jnp.float32


def _cparams(n_grid_axes):
    return pltpu.CompilerParams(dimension_semantics=("arbitrary",) * n_grid_axes,
                                vmem_limit_bytes=VMEM_LIMIT)


def _dot(a, b):
    return jnp.dot(a, b, preferred_element_type=F32)


def _dot_nt(a, b):
    return lax.dot_general(a, b, (((1,), (1,)), ((), ())), preferred_element_type=F32)


def _seg_select(row0, n_rows, n_lat, two_rows):
    rid = row0 + lax.broadcasted_iota(jnp.int32, (n_rows, 1), 0)
    return jnp.where(rid >= n_lat, two_rows[1:2, :], two_rows[0:1, :])


def _mod_kernel(c_ref, w_ref, b_ref, o_ref, s_ref, *, tn):
    k_dim = w_ref.shape[0]

    @pl.when((pl.program_id(0) == 0) & (pl.program_id(1) == 0))
    def _():
        x = c_ref[...]
        s = x * jax.nn.sigmoid(x)
        s_ref[0] = jnp.broadcast_to(s[:, 0:1], (k_dim, LANES))
        s_ref[1] = jnp.broadcast_to(s[:, 1:2], (k_dim, LANES))

    n_groups = tn // LANES

    def body(k, accs):
        r = pl.multiple_of(k * SUBLANES, SUBLANES)
        s0 = s_ref[0, pl.ds(r, SUBLANES), :]
        s1 = s_ref[1, pl.ds(r, SUBLANES), :]
        out = []
        for g in range(n_groups):
            w = w_ref[pl.ds(r, SUBLANES), g * LANES:(g + 1) * LANES]
            out.append(accs[2 * g] + s0 * w)
            out.append(accs[2 * g + 1] + s1 * w)
        return tuple(out)

    zeros = tuple(jnp.zeros((SUBLANES, LANES), F32) for _ in range(2 * n_groups))
    accs = lax.fori_loop(0, k_dim // SUBLANES, body, zeros, unroll=8)
    for g in range(n_groups):
        b = b_ref[:, g * LANES:(g + 1) * LANES]
        o_ref[0:1, g * LANES:(g + 1) * LANES] = jnp.sum(accs[2 * g], axis=0, keepdims=True) + b
        o_ref[1:2, g * LANES:(g + 1) * LANES] = jnp.sum(accs[2 * g + 1], axis=0, keepdims=True) + b


def adaln_modulation(c2, mod_w, mod_b, *, tn=512):
    n_layers, k_dim, n = mod_w.shape
    tn = min(tn, n)
    return pl.pallas_call(
        functools.partial(_mod_kernel, tn=tn),
        out_shape=jax.ShapeDtypeStruct((n_layers, 2, n), F32),
        grid=(n_layers, n // tn),
        in_specs=[
            pl.BlockSpec((k_dim, 2), lambda l, j: (0, 0)),
            pl.BlockSpec((None, k_dim, tn), lambda l, j: (l, 0, j)),
            pl.BlockSpec((None, 1, tn), lambda l, j: (l, 0, j)),
        ],
        out_specs=pl.BlockSpec((None, 2, tn), lambda l, j: (l, 0, j)),
        scratch_shapes=[pltpu.VMEM((2, k_dim, LANES), F32)],
        compiler_params=_cparams(2),
        name="adaln_mod",
    )(c2, mod_w, mod_b.reshape(n_layers, 1, n))


def _norm_mod_kernel(h_ref, w_ref, shift_ref, scale_ref, o_ref, *, n_lat, tm):
    x = h_ref[...]
    y = x * lax.rsqrt(jnp.mean(x * x, axis=-1, keepdims=True) + EPS)
    y = y * w_ref[...]
    row0 = pl.program_id(0) * tm
    scale = _seg_select(row0, tm, n_lat, scale_ref[...])
    shift = _seg_select(row0, tm, n_lat, shift_ref[...])
    o_ref[...] = (y * (1.0 + scale) + shift).astype(o_ref.dtype)


def norm_modulate(h, w, shift2, scale2, *, n_lat, out_dtype, n_rows=None, tm=256):
    m, d = h.shape
    m = m if n_rows is None else n_rows
    tm = min(tm, m)
    return pl.pallas_call(
        functools.partial(_norm_mod_kernel, n_lat=n_lat, tm=tm),
        out_shape=jax.ShapeDtypeStruct((m, d), out_dtype),
        grid=(m // tm,),
        in_specs=[
            pl.BlockSpec((tm, d), lambda i: (i, 0)),
            pl.BlockSpec((1, d), lambda i: (0, 0)),
            pl.BlockSpec((2, d), lambda i: (0, 0)),
            pl.BlockSpec((2, d), lambda i: (0, 0)),
        ],
        out_specs=pl.BlockSpec((tm, d), lambda i: (i, 0)),
        compiler_params=_cparams(1),
        name="norm_modulate",
    )(h, w.reshape(1, d), shift2, scale2)


def _mm_kernel(a_ref, w_ref, o_ref):
    o_ref[...] = _dot(a_ref[...], w_ref[...]).astype(o_ref.dtype)


def matmul(a, w, *, col0, n_cols, out_dtype, tm, tn):
    m, k = a.shape
    blk0 = col0 // tn
    return pl.pallas_call(
        _mm_kernel,
        out_shape=jax.ShapeDtypeStruct((m, n_cols), out_dtype),
        grid=(m // tm, n_cols // tn),
        in_specs=[pl.BlockSpec((tm, k), lambda i, j: (i, 0)),
                  pl.BlockSpec((k, tn), lambda i, j: (0, blk0 + j))],
        out_specs=pl.BlockSpec((tm, tn), lambda i, j: (i, j)),
        compiler_params=_cparams(2),
        name="matmul",
    )(a, w)


def _mm_nt_kernel(wt_ref, a_ref, o_ref):
    o_ref[...] = _dot_nt(wt_ref[...], a_ref[...]).astype(o_ref.dtype)


def matmul_transposed_out(wt, a, *, out_dtype, tm, tn):
    m, k = a.shape
    n = wt.shape[0]
    return pl.pallas_call(
        _mm_nt_kernel,
        out_shape=jax.ShapeDtypeStruct((n, m), out_dtype),
        grid=(m // tm, n // tn),
        in_specs=[pl.BlockSpec((tn, k), lambda i, j: (j, 0)),
                  pl.BlockSpec((tm, k), lambda i, j: (i, 0))],
        out_specs=pl.BlockSpec((tn, tm), lambda i, j: (j, i)),
        compiler_params=_cparams(2),
        name="matmul_transposed_out",
    )(wt, a)


def _mm_rope_kernel(a_ref, w_ref, cos_ref, sa_ref, sb_ref, o_ref, *, tn):
    acc = _dot(a_ref[...], w_ref[...])
    reps = tn // HEAD_DIM
    cos = jnp.concatenate([cos_ref[...]] * reps, axis=1)
    sa = jnp.concatenate([sa_ref[...]] * reps, axis=1)
    sb = jnp.concatenate([sb_ref[...]] * reps, axis=1)
    up = pltpu.roll(acc, tn - HEAD_DIM // 4, 1)
    dn = pltpu.roll(acc, HEAD_DIM // 4, 1)
    o_ref[...] = (acc * cos + up * sa + dn * sb).astype(o_ref.dtype)


def matmul_rope(a, w, cos, sin_a, sin_b, *, n_cols, out_dtype, tm, tn):
    m, k = a.shape
    return pl.pallas_call(
        functools.partial(_mm_rope_kernel, tn=tn),
        out_shape=jax.ShapeDtypeStruct((m, n_cols), out_dtype),
        grid=(m // tm, n_cols // tn),
        in_specs=[pl.BlockSpec((tm, k), lambda i, j: (i, 0)),
                  pl.BlockSpec((k, tn), lambda i, j: (0, j)),
                  pl.BlockSpec((tm, HEAD_DIM), lambda i, j: (i, 0)),
                  pl.BlockSpec((tm, HEAD_DIM), lambda i, j: (i, 0)),
                  pl.BlockSpec((tm, HEAD_DIM), lambda i, j: (i, 0))],
        out_specs=pl.BlockSpec((tm, tn), lambda i, j: (i, j)),
        compiler_params=_cparams(2),
        name="matmul_rope",
    )(a, w, cos, sin_a, sin_b)


def _mm_res_kernel(*refs, n_a, n_lat, tm):
    a_refs = refs[:n_a]
    w_ref, res_ref, gate_ref, o_ref = refs[n_a:]
    acc = None
    k0 = 0
    for a_ref in a_refs:
        kk = a_ref.shape[1]
        part = _dot(a_ref[...], w_ref[k0:k0 + kk, :])
        acc = part if acc is None else acc + part
        k0 += kk
    gate = _seg_select(pl.program_id(0) * tm, tm, n_lat, gate_ref[...])
    o_ref[...] = res_ref[...] + gate * acc


def matmul_residual(a_list, w, res, gate2, *, n_lat, tm, tn):
    m = res.shape[0]
    k, n = w.shape
    in_specs = [pl.BlockSpec((tm, a.shape[1]), lambda i, j: (i, 0)) for a in a_list]
    in_specs += [pl.BlockSpec((k, tn), lambda i, j: (0, j)),
                 pl.BlockSpec((tm, tn), lambda i, j: (i, j)),
                 pl.BlockSpec((2, tn), lambda i, j: (0, j))]
    return pl.pallas_call(
        functools.partial(_mm_res_kernel, n_a=len(a_list), n_lat=n_lat, tm=tm),
        out_shape=jax.ShapeDtypeStruct((m, n), F32),
        grid=(m // tm, n // tn),
        in_specs=in_specs,
        out_specs=pl.BlockSpec((tm, tn), lambda i, j: (i, j)),
        compiler_params=_cparams(2),
        name="matmul_residual",
    )(*a_list, w, res, gate2)


def _na_window_start(t, rows_per_tile, win_rows, n_rows):
    return jnp.clip(t * rows_per_tile - NA_WIN_ROWS // 2, 0, n_rows - win_rows)


def _na_kernel(pat_ref, q_ref, k_ref, v_ref, kc_ref, vc_ref, bias_ref, o_ref, *,
               rows_per_tile, win_rows, n_rows):
    del pat_ref
    t = pl.program_id(1)
    scale = HEAD_DIM ** -0.5
    start = pl.multiple_of(_na_window_start(t, rows_per_tile, win_rows, n_rows) * GRID_W, GRID_W)
    q = q_ref[...]
    k_loc = k_ref[pl.ds(start, win_rows * GRID_W), :]
    v_loc = v_ref[pl.ds(start, win_rows * GRID_W), :]
    s_loc = _dot_nt(q, k_loc) * scale + bias_ref[...]
    s_ctx = _dot_nt(q, kc_ref[...]) * scale
    m = jnp.maximum(jnp.max(s_loc, axis=-1, keepdims=True), jnp.max(s_ctx, axis=-1, keepdims=True))
    p_loc = jnp.exp(s_loc - m)
    p_ctx = jnp.exp(s_ctx - m)
    denom = jnp.sum(p_loc, axis=-1, keepdims=True) + jnp.sum(p_ctx, axis=-1, keepdims=True)
    o = _dot(p_loc.astype(BF16), v_loc) + _dot(p_ctx.astype(BF16), vc_ref[...])
    o_ref[...] = (o / denom).astype(o_ref.dtype)


def _na_tile_patterns(n_rows, rows_per_tile):
    win_rows = rows_per_tile + NA_WIN_ROWS - 1
    keys, pat_of_tile, rep_tile = {}, [], []
    for t in range(n_rows // rows_per_tile):
        r0 = t * rows_per_tile
        us = min(max(r0 - NA_WIN_ROWS // 2, 0), n_rows - win_rows)
        rs = tuple(min(max(r0 + i - NA_WIN_ROWS // 2, 0), n_rows - NA_WIN_ROWS) - us
                   for i in range(rows_per_tile))
        key = (r0 - us, rs)
        if key not in keys:
            keys[key] = len(keys)
            rep_tile.append(t)
        pat_of_tile.append(keys[key])
    return np.asarray(pat_of_tile, np.int32), rep_tile, win_rows


def _na_bias_table(rpb, n_rows, rows_per_tile):
    pat_of_tile, rep_tile, win_rows = _na_tile_patterns(n_rows, rows_per_tile)
    heads = rpb.shape[0]
    w = np.arange(GRID_W)
    col_start = np.clip(w - NA_WIN_COLS // 2, 0, GRID_W - NA_WIN_COLS)
    col_valid = (w[None, :] >= col_start[:, None]) & (w[None, :] < col_start[:, None] + NA_WIN_COLS)
    col_off = w[None, :] - w[:, None] + NA_WIN_COLS - 1
    col_sel = (col_off[:, :, None] == np.arange(2 * NA_WIN_COLS - 1)) & col_valid[:, :, None]
    row_sel, valid = [], []
    for t in rep_tile:
        r0 = t * rows_per_tile
        us = min(max(r0 - NA_WIN_ROWS // 2, 0), n_rows - win_rows)
        r = r0 + np.arange(rows_per_tile)
        rs = np.clip(r - NA_WIN_ROWS // 2, 0, n_rows - NA_WIN_ROWS)
        key_row = us + np.arange(win_rows)
        row_valid = (key_row[None, :] >= rs[:, None]) & (key_row[None, :] < rs[:, None] + NA_WIN_ROWS)
        row_off = key_row[None, :] - r[:, None] + NA_WIN_ROWS - 1
        row_sel.append((row_off[:, :, None] == np.arange(2 * NA_WIN_ROWS - 1)) & row_valid[:, :, None])
        valid.append(row_valid[:, None, :, None] & col_valid[None, :, None, :])
    row_sel = jnp.asarray(np.stack(row_sel), F32)
    valid = jnp.asarray(np.stack(valid))
    toeplitz = jnp.einsum('hrc,wvc->hrwv', rpb.astype(F32), jnp.asarray(col_sel, F32),
                          precision=lax.Precision.HIGHEST)
    bias = jnp.einsum('pijr,hrwv->phiwjv', row_sel, toeplitz, precision=lax.Precision.HIGHEST)
    bias = jnp.where(valid[:, None], bias, NEG_INF)
    bias = bias.reshape(len(rep_tile), heads, rows_per_tile * GRID_W, win_rows * GRID_W)
    return bias, pat_of_tile, win_rows


def neighborhood_attention(qkv, rpb, *, n_lat, n_ctx, rows_per_tile=4):
    heads = rpb.shape[0]
    n_rows = n_lat // GRID_W
    bias, pat_of_tile, win_rows = _na_bias_table(rpb, n_rows, rows_per_tile)
    tq = rows_per_tile * GRID_W
    ctx_blk = n_lat // n_ctx
    grid_spec = pltpu.PrefetchScalarGridSpec(
        num_scalar_prefetch=1,
        grid=(heads, n_rows // rows_per_tile),
        in_specs=[
            pl.BlockSpec((tq, HEAD_DIM), lambda h, t, pat: (t, h)),
            pl.BlockSpec((n_lat, HEAD_DIM), lambda h, t, pat: (0, heads + h)),
            pl.BlockSpec((n_lat, HEAD_DIM), lambda h, t, pat: (0, 2 * heads + h)),
            pl.BlockSpec((n_ctx, HEAD_DIM), lambda h, t, pat: (ctx_blk, heads + h)),
            pl.BlockSpec((n_ctx, HEAD_DIM), lambda h, t, pat: (ctx_blk, 2 * heads + h)),
            pl.BlockSpec((None, None, tq, win_rows * GRID_W), lambda h, t, pat: (pat[t], h, 0, 0)),
        ],
        out_specs=pl.BlockSpec((tq, HEAD_DIM), lambda h, t, pat: (t, h)),
    )
    return pl.pallas_call(
        functools.partial(_na_kernel, rows_per_tile=rows_per_tile, win_rows=win_rows, n_rows=n_rows),
        out_shape=jax.ShapeDtypeStruct((n_lat, heads * HEAD_DIM), BF16),
        grid_spec=grid_spec,
        compiler_params=_cparams(2),
        name="neighborhood_attention",
    )(jnp.asarray(pat_of_tile), qkv, qkv, qkv, qkv, qkv, bias)


def _ctx_attn_kernel(q_ref, k_ref, v_ref, o_ref):
    s = _dot_nt(q_ref[...], k_ref[...]) * (HEAD_DIM ** -0.5)
    p = jnp.exp(s - jnp.max(s, axis=-1, keepdims=True))
    denom = jnp.sum(p, axis=-1, keepdims=True)
    o_ref[...] = (_dot(p.astype(BF16), v_ref[...]) / denom).astype(o_ref.dtype)


def context_attention(qkv, *, heads, n_lat, n_ctx):
    ctx_blk = n_lat // n_ctx
    return pl.pallas_call(
        _ctx_attn_kernel,
        out_shape=jax.ShapeDtypeStruct((n_ctx, heads * HEAD_DIM), BF16),
        grid=(heads,),
        in_specs=[pl.BlockSpec((n_ctx, HEAD_DIM), lambda h: (ctx_blk, h)),
                  pl.BlockSpec((n_ctx, HEAD_DIM), lambda h: (ctx_blk, heads + h)),
                  pl.BlockSpec((n_ctx, HEAD_DIM), lambda h: (ctx_blk, 2 * heads + h))],
        out_specs=pl.BlockSpec((n_ctx, HEAD_DIM), lambda h: (0, h)),
        compiler_params=_cparams(1),
        name="context_attention",
    )(qkv, qkv, qkv)


CONV_HALO = 16


def _conformer_kernel(ap_ref, gp_ref, a_ref, g_ref, an_ref, gn_ref, cw_ref, cb_ref, lw_ref, lb_ref, o_ref,
                      buf_ref, y_ref, *, tm, first_tiles, last_tiles):
    i = pl.program_id(0)
    ch = a_ref.shape[1]
    has_prev = jnp.logical_not(functools.reduce(jnp.logical_or, [i == t for t in first_tiles]))
    has_next = jnp.logical_not(functools.reduce(jnp.logical_or, [i == t for t in last_tiles]))

    def glu(a, g):
        return a * jax.nn.sigmoid(g)

    buf_ref[0:CONV_HALO, :] = jnp.where(has_prev, glu(ap_ref[...], gp_ref[...]), 0.0)
    buf_ref[CONV_HALO:CONV_HALO + tm, :] = glu(a_ref[...], g_ref[...])
    buf_ref[CONV_HALO + tm:, :] = jnp.where(has_next, glu(an_ref[...], gn_ref[...]), 0.0)

    base = CONV_HALO - CONV_WIDTH // 2

    def chunk(c, carry):
        col = pl.multiple_of(c * LANES, LANES)
        acc = jnp.zeros((tm, LANES), F32) + cb_ref[:, pl.ds(col, LANES)]
        for tap in range(CONV_WIDTH):
            acc = acc + buf_ref[base + tap:base + tap + tm, pl.ds(col, LANES)] * cw_ref[tap:tap + 1, pl.ds(col, LANES)]
        y_ref[:, pl.ds(col, LANES)] = acc
        return carry

    lax.fori_loop(0, ch // LANES, chunk, 0)

    y = y_ref[...]
    mu = jnp.mean(y, axis=-1, keepdims=True)
    yc = y - mu
    z = yc * lax.rsqrt(jnp.mean(yc * yc, axis=-1, keepdims=True) + EPS)
    z = z * lw_ref[...] + lb_ref[...]
    o_ref[...] = (z * jax.nn.sigmoid(z)).astype(o_ref.dtype)


def conformer_conv(ag, cv_w, cv_b, ln_w, ln_b, *, n_lat, tm=256):
    m, two_c = ag.shape
    ch = two_c // 2
    n_tiles = m // tm
    hb = tm // CONV_HALO
    n_hblk = m // CONV_HALO
    first_tiles = (0, n_lat // tm)
    last_tiles = (n_lat // tm - 1, n_tiles - 1)
    prev_map = lambda c: (lambda i: (jnp.maximum(i * hb - 1, 0), c))
    next_map = lambda c: (lambda i: (jnp.minimum((i + 1) * hb, n_hblk - 1), c))
    cur_map = lambda c: (lambda i: (i, c))
    vec = lambda: pl.BlockSpec((1, ch), lambda i: (0, 0))
    return pl.pallas_call(
        functools.partial(_conformer_kernel, tm=tm, first_tiles=first_tiles, last_tiles=last_tiles),
        out_shape=jax.ShapeDtypeStruct((m, ch), BF16),
        grid=(n_tiles,),
        in_specs=[pl.BlockSpec((CONV_HALO, ch), prev_map(0)), pl.BlockSpec((CONV_HALO, ch), prev_map(1)),
                  pl.BlockSpec((tm, ch), cur_map(0)), pl.BlockSpec((tm, ch), cur_map(1)),
                  pl.BlockSpec((CONV_HALO, ch), next_map(0)), pl.BlockSpec((CONV_HALO, ch), next_map(1)),
                  pl.BlockSpec((CONV_WIDTH, ch), lambda i: (0, 0)), vec(), vec(), vec()],
        out_specs=pl.BlockSpec((tm, ch), lambda i: (i, 0)),
        scratch_shapes=[pltpu.VMEM((tm + 2 * CONV_HALO, ch), F32), pltpu.VMEM((tm, ch), F32)],
        compiler_params=_cparams(1),
        name="conformer_conv",
    )(ag, ag, ag, ag, ag, ag, cv_w, cv_b.reshape(1, ch), ln_w.reshape(1, ch), ln_b.reshape(1, ch))


DIFF_Q_TILE = 256
DIFF_KV_CHUNK = 768
LOG2E = 1.4426950408889634


def _diff_attn_kernel(lam_ref, q_ref, k_ref, vt_ref, sw_ref, o_ref, acc1_ref, acc2_ref, *,
                      key_row0, n_chunks, tk, lambda_init):
    c = HEAD_DIM ** -0.5 * LOG2E
    q = (q_ref[:, :HEAD_DIM], q_ref[:, HEAD_DIM:])
    acc_refs = (acc1_ref, acc2_ref)
    tq = q_ref.shape[0]
    m = [jnp.full((1, tq), NEG_INF, F32)] * 2
    l = [jnp.zeros((1, tq), F32)] * 2

    def scores(ci):
        kc = k_ref[key_row0 + ci * tk:key_row0 + (ci + 1) * tk, :]
        return [_dot_nt(kc[:, b * HEAD_DIM:(b + 1) * HEAD_DIM], q[b]) * c for b in range(2)]

    def softmax_step(s, b):
        m_new = jnp.maximum(m[b], jnp.max(s, axis=0, keepdims=True))
        alpha = jnp.exp2(m[b] - m_new)
        p = jnp.exp2(s - m_new)
        l[b] = alpha * l[b] + jnp.sum(p, axis=0, keepdims=True)
        m[b] = m_new
        return p.astype(BF16), alpha

    def values(ci, p, alpha, b):
        vtc = vt_ref[:, key_row0 + ci * tk:key_row0 + (ci + 1) * tk]
        pv = _dot(vtc, p)
        acc_refs[b][...] = pv if ci == 0 else alpha * acc_refs[b][...] + pv

    s_of, p_of = {}, {}
    for t in range(n_chunks + 2):
        if t < n_chunks:
            s_of[t] = scores(t)
        if 0 <= t - 1 < n_chunks:
            p_of[t - 1] = [softmax_step(s, b) for b, s in enumerate(s_of.pop(t - 1))]
        if 0 <= t - 2 < n_chunks:
            for b, (p, alpha) in enumerate(p_of.pop(t - 2)):
                values(t - 2, p, alpha, b)

    lp = lam_ref[...]
    lam = (jnp.exp(jnp.sum(lp[0:1] * lp[1:2], axis=-1, keepdims=True))
           - jnp.exp(jnp.sum(lp[2:3] * lp[3:4], axis=-1, keepdims=True)) + lambda_init)
    o = acc1_ref[...] * (1.0 / l[0]) - acc2_ref[...] * (lam / l[1])
    o = o * lax.rsqrt(jnp.mean(o * o, axis=0, keepdims=True) + EPS)
    o = (o * sw_ref[...]) * (1.0 - lambda_init)
    o_ref[...] = o.T.astype(o_ref.dtype)


def _kv_chunk(n_keys):
    return max(t for t in range(MXU_DIM, DIFF_KV_CHUNK + 1, MXU_DIM) if n_keys % t == 0)


def diff_attention(qk, vt, lam_p, subln_w, *, heads, q_row0, n_q, key_row0, n_keys, lambda_init, tq):
    m = qk.shape[0]
    hd2 = 2 * HEAD_DIM
    q_blk0 = q_row0 // tq
    tk = _kv_chunk(n_keys)
    scratch = [pltpu.VMEM((hd2, tq), F32), pltpu.VMEM((hd2, tq), F32)]
    return pl.pallas_call(
        functools.partial(_diff_attn_kernel, key_row0=key_row0, n_chunks=n_keys // tk, tk=tk,
                          lambda_init=lambda_init),
        out_shape=jax.ShapeDtypeStruct((n_q, heads * hd2), BF16),
        grid=(heads, n_q // tq),
        in_specs=[pl.BlockSpec((4, HEAD_DIM), lambda h, i: (0, 0)),
                  pl.BlockSpec((tq, hd2), lambda h, i: (q_blk0 + i, h)),
                  pl.BlockSpec((m, hd2), lambda h, i: (0, heads + h)),
                  pl.BlockSpec((hd2, m), lambda h, i: (h, 0)),
                  pl.BlockSpec((hd2, 1), lambda h, i: (0, 0))],
        out_specs=pl.BlockSpec((tq, hd2), lambda h, i: (i, h)),
        scratch_shapes=scratch,
        compiler_params=_cparams(2),
        name="diff_attention",
    )(lam_p.astype(F32), qk, qk, vt, subln_w.reshape(hd2, 1))


FFN_HALO = 16


def _ffn_up_kernel(ap_ref, a_ref, an_ref, wg_ref, wv_ref, cwg_ref, cbg_ref, cwv_ref, cbv_ref, o_ref,
                   aext_ref, ug_ref, uv_ref, *, tm, seg_starts, seg_ends):
    @pl.when(pl.program_id(1) == 0)
    def _():
        aext_ref[0:FFN_HALO, :] = ap_ref[...]
        aext_ref[FFN_HALO:FFN_HALO + tm, :] = a_ref[...]
        aext_ref[FFN_HALO + tm:, :] = an_ref[...]

    a_ext = aext_ref[...]
    ug_ref[...] = _dot(a_ext, wg_ref[...])
    uv_ref[...] = _dot(a_ext, wv_ref[...])

    rid = pl.program_id(0) * tm + lax.broadcasted_iota(jnp.int32, (tm, 1), 0)
    has_prev = functools.reduce(jnp.logical_and, [rid != r for r in seg_starts])
    has_next = functools.reduce(jnp.logical_and, [rid != r for r in seg_ends])

    def conv(u_ref, w_ref, b_ref):
        prev = jnp.where(has_prev, u_ref[FFN_HALO - 1:FFN_HALO - 1 + tm, :], 0.0)
        nxt = jnp.where(has_next, u_ref[FFN_HALO + 1:FFN_HALO + 1 + tm, :], 0.0)
        return (b_ref[...] + prev * w_ref[0:1, :] + u_ref[FFN_HALO:FFN_HALO + tm, :] * w_ref[1:2, :]
                + nxt * w_ref[2:3, :])

    gate = conv(ug_ref, cwg_ref, cbg_ref)
    val = conv(uv_ref, cwv_ref, cbv_ref)
    o_ref[...] = (gate * jax.nn.sigmoid(gate) * val).astype(o_ref.dtype)


def ffn_up_conv_gate(a, w_up, conv_w, conv_b, *, n_lat, tm, tn):
    m, k = a.shape
    two_f = w_up.shape[1]
    f = two_f // 2
    hb = tm // FFN_HALO
    n_hblk = m // FFN_HALO
    voff = f // tn
    b2 = conv_b.reshape(1, two_f)
    return pl.pallas_call(
        functools.partial(_ffn_up_kernel, tm=tm, seg_starts=(0, n_lat), seg_ends=(n_lat - 1, m - 1)),
        out_shape=jax.ShapeDtypeStruct((m, f), BF16),
        grid=(m // tm, f // tn),
        in_specs=[pl.BlockSpec((FFN_HALO, k), lambda i, j: (jnp.maximum(i * hb - 1, 0), 0)),
                  pl.BlockSpec((tm, k), lambda i, j: (i, 0)),
                  pl.BlockSpec((FFN_HALO, k), lambda i, j: (jnp.minimum((i + 1) * hb, n_hblk - 1), 0)),
                  pl.BlockSpec((k, tn), lambda i, j: (0, j)),
                  pl.BlockSpec((k, tn), lambda i, j: (0, j + voff)),
                  pl.BlockSpec((FFN_CONV_WIDTH, tn), lambda i, j: (0, j)),
                  pl.BlockSpec((1, tn), lambda i, j: (0, j)),
                  pl.BlockSpec((FFN_CONV_WIDTH, tn), lambda i, j: (0, j + voff)),
                  pl.BlockSpec((1, tn), lambda i, j: (0, j + voff))],
        out_specs=pl.BlockSpec((tm, tn), lambda i, j: (i, j)),
        scratch_shapes=[pltpu.VMEM((tm + 2 * FFN_HALO, k), BF16),
                        pltpu.VMEM((tm + 2 * FFN_HALO, tn), F32),
                        pltpu.VMEM((tm + 2 * FFN_HALO, tn), F32)],
        compiler_params=_cparams(2),
        name="ffn_up_conv_gate",
    )(a, a, a, w_up, w_up, conv_w, b2, conv_w, b2)


def _rope_tables(n_lat, n_ctx):
    t = jnp.arange(n_lat)
    row = (t // GRID_W).astype(F32)
    col = (t % GRID_W).astype(F32)
    per_axis = HEAD_DIM // 2
    inv = ROPE_BASE ** (-jnp.arange(0, per_axis, 2, dtype=F32) / per_axis)
    ang_r, ang_c = row[:, None] * inv, col[:, None] * inv
    cr, sr, cc, sc = jnp.cos(ang_r), jnp.sin(ang_r), jnp.cos(ang_c), jnp.sin(ang_c)
    zero = jnp.zeros_like(sr)
    cos = jnp.concatenate([cr, cr, cc, cc], axis=1)
    sin_a = jnp.concatenate([-sr, zero, -sc, zero], axis=1)
    sin_b = jnp.concatenate([zero, sr, zero, sc], axis=1)
    pad = lambda x, v: jnp.concatenate([x, jnp.full((n_ctx, HEAD_DIM), v, F32)], axis=0)
    return pad(cos, 1.0), pad(sin_a, 0.0), pad(sin_b, 0.0)


MXU_DIM = 256
MATMUL_ROW_TILE = 768
MATMUL_COL_TILE = 512


def _matmul_tiles(m):
    tm = max(t for t in range(MXU_DIM, MATMUL_ROW_TILE + 1, MXU_DIM) if m % t == 0)
    return tm, MATMUL_COL_TILE


def kernel(x, c, ctx, c_ctx, mod_w, mod_b, norm1_w, norm2_w, na_w_in, na_w_out, na_rpb, cv_w, cv_b, cv_ln_w,
           cv_ln_b, diff_w_in, diff_w_out, diff_lambda, diff_subln_w, ffn_w_up, ffn_conv_w, ffn_conv_b,
           ffn_w_down, final_norm_w):
    assert x.shape[0] == 1 and ctx.shape[0] == 1
    n_lat, d = x.shape[1], x.shape[2]
    n_ctx = ctx.shape[1]
    depth = mod_w.shape[0]
    m = n_lat + n_ctx
    na_heads = na_rpb.shape[1]
    na_width = na_heads * HEAD_DIM
    diff_heads = diff_w_in.shape[2] // (3 * 2 * HEAD_DIM)
    diff_width = diff_heads * 2 * HEAD_DIM
    tm, tn = _matmul_tiles(m)

    h = jnp.concatenate([x[0], ctx[0]], axis=0)
    mod = adaln_modulation(jnp.stack([c[0], c_ctx], axis=1), mod_w, mod_b)
    mod = mod.reshape(depth, 2, 6, d)
    cos, sin_a, sin_b = _rope_tables(n_lat, n_ctx)

    for l in range(depth):
        i = l // 2
        shift1, scale1, gate1, shift2, scale2, gate2 = (mod[l, :, j, :] for j in range(6))
        a = norm_modulate(h, norm1_w[l], shift1, scale1, n_lat=n_lat, out_dtype=BF16)
        if l % 2 == 0:
            w_in = na_w_in[i].astype(BF16)
            qkv = matmul(a, w_in, col0=0, n_cols=3 * na_width, out_dtype=BF16, tm=tm, tn=tn)
            ag = matmul(a, w_in, col0=3 * na_width, n_cols=w_in.shape[1] - 3 * na_width, out_dtype=F32,
                        tm=tm, tn=tn)
            o_na_lat = neighborhood_attention(qkv, na_rpb[i], n_lat=n_lat, n_ctx=n_ctx)
            o_na_ctx = context_attention(qkv, heads=na_heads, n_lat=n_lat, n_ctx=n_ctx)
            o_na = jnp.concatenate([o_na_lat, o_na_ctx], axis=0)
            o_cv = conformer_conv(ag, cv_w[i], cv_b[i], cv_ln_w[i], cv_ln_b[i], n_lat=n_lat)
            h = matmul_residual([o_na, o_cv], na_w_out[i].astype(BF16), h, gate1, n_lat=n_lat, tm=tm, tn=tn)
        else:
            lambda_init = 0.8 - 0.6 * math.exp(-0.3 * l)
            qk = matmul_rope(a, diff_w_in[i][:, :2 * diff_width].astype(BF16), cos, sin_a, sin_b,
                             n_cols=2 * diff_width, out_dtype=BF16, tm=tm, tn=tn)
            w_v_t = diff_w_in[i][:, 2 * diff_width:].T.astype(BF16)
            vt = matmul_transposed_out(w_v_t, a, out_dtype=BF16, tm=tm, tn=tn)
            o_lat = diff_attention(qk, vt, diff_lambda[i], diff_subln_w[i], heads=diff_heads, q_row0=0, n_q=n_lat,
                                   key_row0=0, n_keys=m, lambda_init=lambda_init, tq=min(DIFF_Q_TILE, n_lat))
            o_ctx = diff_attention(qk, vt, diff_lambda[i], diff_subln_w[i], heads=diff_heads, q_row0=n_lat,
                                   n_q=n_ctx, key_row0=n_lat, n_keys=n_ctx, lambda_init=lambda_init, tq=n_ctx)
            o = jnp.concatenate([o_lat, o_ctx], axis=0)
            h = matmul_residual([o], diff_w_out[i].astype(BF16), h, gate1, n_lat=n_lat, tm=tm, tn=tn)
        f = norm_modulate(h, norm2_w[l], shift2, scale2, n_lat=n_lat, out_dtype=BF16)
        act = ffn_up_conv_gate(f, ffn_w_up[l].astype(BF16), ffn_conv_w[l], ffn_conv_b[l], n_lat=n_lat,
                               tm=tm, tn=tn)
        h = matmul_residual([act], ffn_w_down[l].astype(BF16), h, gate2, n_lat=n_lat, tm=tm, tn=tn)

    zeros2 = jnp.zeros((2, d), F32)
    out = norm_modulate(h, final_norm_w, zeros2, zeros2, n_lat=n_lat, out_dtype=F32, n_rows=n_lat)
    return out[None]
```

```python
import functools
import math

import jax
import jax.numpy as jnp
import numpy as np
from jax import lax
from jax.experimental import pallas as pl
from jax.experimental.pallas import tpu as pltpu

GRID_W = 64
HEAD_DIM = 128
NA_WIN_ROWS = 8
NA_WIN_COLS = 16
CONV_WIDTH = 31
FFN_CONV_WIDTH = 3
ROPE_BASE = 10000.0
EPS = 1e-6

V7X_VMEM_BYTES = 64 * 1024 * 1024
VMEM_LIMIT = V7X_VMEM_BYTES * 7 // 8
LANES = 128
SUBLANES = 8
NEG_INF = -1e30

BF16 = jnp.bfloat16
F32 = jnp.float32


def _cparams(n_grid_axes):
    return pltpu.CompilerParams(dimension_semantics=("arbitrary",) * n_grid_axes,
                                vmem_limit_bytes=VMEM_LIMIT)


def _dot(a, b):
    return jnp.dot(a, b, preferred_element_type=F32)


def _dot_nt(a, b):
    return lax.dot_general(a, b, (((1,), (1,)), ((), ())), preferred_element_type=F32)


def _seg_select(row0, n_rows, n_lat, two_rows):
    rid = row0 + lax.broadcasted_iota(jnp.int32, (n_rows, 1), 0)
    return jnp.where(rid >= n_lat, two_rows[1:2, :], two_rows[0:1, :])


def _mod_kernel(c_ref, w_ref, b_ref, o_ref, s_ref, *, tn):
    k_dim = w_ref.shape[0]

    @pl.when((pl.program_id(0) == 0) & (pl.program_id(1) == 0))
    def _():
        x = c_ref[...]
        s = x * jax.nn.sigmoid(x)
        s_ref[0] = jnp.broadcast_to(s[:, 0:1], (k_dim, LANES))
        s_ref[1] = jnp.broadcast_to(s[:, 1:2], (k_dim, LANES))

    n_groups = tn // LANES

    def body(k, accs):
        r = pl.multiple_of(k * SUBLANES, SUBLANES)
        s0 = s_ref[0, pl.ds(r, SUBLANES), :]
        s1 = s_ref[1, pl.ds(r, SUBLANES), :]
        out = []
        for g in range(n_groups):
            w = w_ref[pl.ds(r, SUBLANES), g * LANES:(g + 1) * LANES]
            out.append(accs[2 * g] + s0 * w)
            out.append(accs[2 * g + 1] + s1 * w)
        return tuple(out)

    zeros = tuple(jnp.zeros((SUBLANES, LANES), F32) for _ in range(2 * n_groups))
    accs = lax.fori_loop(0, k_dim // SUBLANES, body, zeros, unroll=8)
    for g in range(n_groups):
        b = b_ref[:, g * LANES:(g + 1) * LANES]
        o_ref[0:1, g * LANES:(g + 1) * LANES] = jnp.sum(accs[2 * g], axis=0, keepdims=True) + b
        o_ref[1:2, g * LANES:(g + 1) * LANES] = jnp.sum(accs[2 * g + 1], axis=0, keepdims=True) + b


def adaln_modulation(c2, mod_w, mod_b, *, tn=512):
    n_layers, k_dim, n = mod_w.shape
    tn = min(tn, n)
    return pl.pallas_call(
        functools.partial(_mod_kernel, tn=tn),
        out_shape=jax.ShapeDtypeStruct((n_layers, 2, n), F32),
        grid=(n_layers, n // tn),
        in_specs=[
            pl.BlockSpec((k_dim, 2), lambda l, j: (0, 0)),
            pl.BlockSpec((None, k_dim, tn), lambda l, j: (l, 0, j)),
            pl.BlockSpec((None, 1, tn), lambda l, j: (l, 0, j)),
        ],
        out_specs=pl.BlockSpec((None, 2, tn), lambda l, j: (l, 0, j)),
        scratch_shapes=[pltpu.VMEM((2, k_dim, LANES), F32)],
        compiler_params=_cparams(2),
        name="adaln_mod",
    )(c2, mod_w, mod_b.reshape(n_layers, 1, n))


def _norm_mod_kernel(h_ref, w_ref, shift_ref, scale_ref, o_ref, *, n_lat, tm):
    x = h_ref[...]
    y = x * lax.rsqrt(jnp.mean(x * x, axis=-1, keepdims=True) + EPS)
    y = y * w_ref[...]
    row0 = pl.program_id(0) * tm
    scale = _seg_select(row0, tm, n_lat, scale_ref[...])
    shift = _seg_select(row0, tm, n_lat, shift_ref[...])
    o_ref[...] = (y * (1.0 + scale) + shift).astype(o_ref.dtype)


def norm_modulate(h, w, shift2, scale2, *, n_lat, out_dtype, n_rows=None, tm=256):
    m, d = h.shape
    m = m if n_rows is None else n_rows
    tm = min(tm, m)
    return pl.pallas_call(
        functools.partial(_norm_mod_kernel, n_lat=n_lat, tm=tm),
        out_shape=jax.ShapeDtypeStruct((m, d), out_dtype),
        grid=(m // tm,),
        in_specs=[
            pl.BlockSpec((tm, d), lambda i: (i, 0)),
            pl.BlockSpec((1, d), lambda i: (0, 0)),
            pl.BlockSpec((2, d), lambda i: (0, 0)),
            pl.BlockSpec((2, d), lambda i: (0, 0)),
        ],
        out_specs=pl.BlockSpec((tm, d), lambda i: (i, 0)),
        compiler_params=_cparams(1),
        name="norm_modulate",
    )(h, w.reshape(1, d), shift2, scale2)


MXU_DIM = 256
ROW_ALIGN = 128
COL_TILES = (512, 256)


def _pick_tiles(m, n, fixed_bytes, row_bytes, col_tiles=COL_TILES):
    best = None
    for tn in col_tiles:
        if n % tn:
            continue
        fits = [t for t in range(ROW_ALIGN, m + 1, ROW_ALIGN)
                if m % t == 0 and fixed_bytes(tn) + t * row_bytes(tn) <= VMEM_LIMIT]
        if fits and (best is None or max(fits) > best[0]):
            best = (max(fits), tn)
    assert best is not None, "no tile fits the VMEM limit"
    return best


def _f32_weight_tile_bytes(k, tn):
    return 2 * k * tn * 4 + k * tn * 2


def _mm_kernel(a_ref, w_ref, o_ref):
    o_ref[...] = _dot(a_ref[...], w_ref[...].astype(BF16)).astype(o_ref.dtype)


def matmul(a, w, layer, *, col0, n_cols, out_dtype):
    m, k = a.shape
    ob = jnp.dtype(out_dtype).itemsize
    tm, tn = _pick_tiles(m, math.gcd(n_cols, col0) if col0 else n_cols,
                         lambda tn: _f32_weight_tile_bytes(k, tn),
                         lambda tn: 2 * k * 2 + 2 * tn * ob + tn * 4)
    blk0 = col0 // tn
    return pl.pallas_call(
        _mm_kernel,
        out_shape=jax.ShapeDtypeStruct((m, n_cols), out_dtype),
        grid=(m // tm, n_cols // tn),
        in_specs=[pl.BlockSpec((tm, k), lambda i, j: (i, 0)),
                  pl.BlockSpec((None, k, tn), lambda i, j: (layer, 0, blk0 + j))],
        out_specs=pl.BlockSpec((tm, tn), lambda i, j: (i, j)),
        compiler_params=_cparams(2),
        name="matmul",
    )(a, w)


def _mm_nt_kernel(wt_ref, a_ref, o_ref):
    o_ref[...] = _dot_nt(wt_ref[...], a_ref[...]).astype(o_ref.dtype)


def matmul_transposed_out(wt, a, *, out_dtype):
    m, k = a.shape
    n = wt.shape[0]
    ob = jnp.dtype(out_dtype).itemsize
    tm, tn = _pick_tiles(m, n, lambda tn: 2 * k * tn * 2, lambda tn: 2 * k * 2 + 2 * tn * ob + tn * 4)
    return pl.pallas_call(
        _mm_nt_kernel,
        out_shape=jax.ShapeDtypeStruct((n, m), out_dtype),
        grid=(m // tm, n // tn),
        in_specs=[pl.BlockSpec((tn, k), lambda i, j: (j, 0)),
                  pl.BlockSpec((tm, k), lambda i, j: (i, 0))],
        out_specs=pl.BlockSpec((tn, tm), lambda i, j: (j, i)),
        compiler_params=_cparams(2),
        name="matmul_transposed_out",
    )(wt, a)


def _mm_rope_kernel(a_ref, w_ref, cos_ref, sa_ref, sb_ref, o_ref, *, tn, scaled_tiles, scale):
    acc = _dot(a_ref[...], w_ref[...].astype(BF16))
    reps = tn // HEAD_DIM
    cos = jnp.concatenate([cos_ref[...]] * reps, axis=1)
    sa = jnp.concatenate([sa_ref[...]] * reps, axis=1)
    sb = jnp.concatenate([sb_ref[...]] * reps, axis=1)
    up = pltpu.roll(acc, tn - HEAD_DIM // 4, 1)
    dn = pltpu.roll(acc, HEAD_DIM // 4, 1)
    out = acc * cos + up * sa + dn * sb
    out = out * jnp.where(pl.program_id(1) < scaled_tiles, scale, 1.0)
    o_ref[...] = out.astype(o_ref.dtype)


def matmul_rope(a, w, layer, cos, sin_a, sin_b, *, n_cols, scaled_cols, scale, out_dtype):
    m, k = a.shape
    ob = jnp.dtype(out_dtype).itemsize
    tm, tn = _pick_tiles(m, math.gcd(n_cols, scaled_cols),
                         lambda tn: _f32_weight_tile_bytes(k, tn),
                         lambda tn: 2 * k * 2 + 2 * tn * ob + 3 * 2 * HEAD_DIM * 4 + 4 * tn * 4)
    return pl.pallas_call(
        functools.partial(_mm_rope_kernel, tn=tn, scaled_tiles=scaled_cols // tn, scale=scale),
        out_shape=jax.ShapeDtypeStruct((m, n_cols), out_dtype),
        grid=(m // tm, n_cols // tn),
        in_specs=[pl.BlockSpec((tm, k), lambda i, j: (i, 0)),
                  pl.BlockSpec((None, k, tn), lambda i, j: (layer, 0, j)),
                  pl.BlockSpec((tm, HEAD_DIM), lambda i, j: (i, 0)),
                  pl.BlockSpec((tm, HEAD_DIM), lambda i, j: (i, 0)),
                  pl.BlockSpec((tm, HEAD_DIM), lambda i, j: (i, 0))],
        out_specs=pl.BlockSpec((tm, tn), lambda i, j: (i, j)),
        compiler_params=_cparams(2),
        name="matmul_rope",
    )(a, w, cos, sin_a, sin_b)


def _mm_res_kernel(*refs, n_a, n_lat, tm):
    a_refs = refs[:n_a]
    w_ref, res_ref, gate_ref, o_ref = refs[n_a:]
    acc = None
    k0 = 0
    for a_ref in a_refs:
        kk = a_ref.shape[1]
        part = _dot(a_ref[...], w_ref[k0:k0 + kk, :].astype(BF16))
        acc = part if acc is None else acc + part
        k0 += kk
    gate = _seg_select(pl.program_id(0) * tm, tm, n_lat, gate_ref[...])
    o_ref[...] = res_ref[...] + gate * acc


def matmul_residual(a_list, w, layer, res, gate2, *, n_lat):
    m = res.shape[0]
    _, k, n = w.shape
    tm, tn = _pick_tiles(m, n, lambda tn: _f32_weight_tile_bytes(k, tn),
                         lambda tn: 2 * k * 2 + 2 * tn * 4 + 2 * tn * 4 + tn * 4)
    in_specs = [pl.BlockSpec((tm, a.shape[1]), lambda i, j: (i, 0)) for a in a_list]
    in_specs += [pl.BlockSpec((None, k, tn), lambda i, j: (layer, 0, j)),
                 pl.BlockSpec((tm, tn), lambda i, j: (i, j)),
                 pl.BlockSpec((2, tn), lambda i, j: (0, j))]
    return pl.pallas_call(
        functools.partial(_mm_res_kernel, n_a=len(a_list), n_lat=n_lat, tm=tm),
        out_shape=jax.ShapeDtypeStruct((m, n), F32),
        grid=(m // tm, n // tn),
        in_specs=in_specs,
        out_specs=pl.BlockSpec((tm, tn), lambda i, j: (i, j)),
        compiler_params=_cparams(2),
        name="matmul_residual",
    )(*a_list, w, res, gate2)


def _na_window_start(t, rows_per_tile, win_rows, n_rows):
    return jnp.clip(t * rows_per_tile - NA_WIN_ROWS // 2, 0, n_rows - win_rows)


def _na_kernel(pat_ref, q_ref, k_ref, v_ref, kc_ref, vc_ref, bias_ref, o_ref, *,
               rows_per_tile, win_rows, n_rows):
    del pat_ref
    t = pl.program_id(1)
    scale = HEAD_DIM ** -0.5
    start = pl.multiple_of(_na_window_start(t, rows_per_tile, win_rows, n_rows) * GRID_W, GRID_W)
    q = q_ref[...]
    k_loc = k_ref[pl.ds(start, win_rows * GRID_W), :]
    v_loc = v_ref[pl.ds(start, win_rows * GRID_W), :]
    bias = jnp.where(t == n_rows // rows_per_tile, NEG_INF, bias_ref[...])
    s_loc = _dot_nt(q, k_loc) * scale + bias
    s_ctx = _dot_nt(q, kc_ref[...]) * scale
    m = jnp.maximum(jnp.max(s_loc, axis=-1, keepdims=True), jnp.max(s_ctx, axis=-1, keepdims=True))
    p_loc = jnp.exp(s_loc - m)
    p_ctx = jnp.exp(s_ctx - m)
    denom = jnp.sum(p_loc, axis=-1, keepdims=True) + jnp.sum(p_ctx, axis=-1, keepdims=True)
    o = _dot(p_loc.astype(BF16), v_loc) + _dot(p_ctx.astype(BF16), vc_ref[...])
    o_ref[...] = (o / denom).astype(o_ref.dtype)


def _na_tile_patterns(n_rows, rows_per_tile):
    win_rows = rows_per_tile + NA_WIN_ROWS - 1
    keys, pat_of_tile, rep_tile = {}, [], []
    for t in range(n_rows // rows_per_tile):
        r0 = t * rows_per_tile
        us = min(max(r0 - NA_WIN_ROWS // 2, 0), n_rows - win_rows)
        rs = tuple(min(max(r0 + i - NA_WIN_ROWS // 2, 0), n_rows - NA_WIN_ROWS) - us
                   for i in range(rows_per_tile))
        key = (r0 - us, rs)
        if key not in keys:
            keys[key] = len(keys)
            rep_tile.append(t)
        pat_of_tile.append(keys[key])
    return np.asarray(pat_of_tile, np.int32), rep_tile, win_rows


def _na_bias_table(rpb, n_rows, rows_per_tile):
    pat_of_tile, rep_tile, win_rows = _na_tile_patterns(n_rows, rows_per_tile)
    heads = rpb.shape[0]
    w = np.arange(GRID_W)
    col_start = np.clip(w - NA_WIN_COLS // 2, 0, GRID_W - NA_WIN_COLS)
    col_valid = (w[None, :] >= col_start[:, None]) & (w[None, :] < col_start[:, None] + NA_WIN_COLS)
    col_off = w[None, :] - w[:, None] + NA_WIN_COLS - 1
    col_sel = (col_off[:, :, None] == np.arange(2 * NA_WIN_COLS - 1)) & col_valid[:, :, None]
    row_sel, valid = [], []
    for t in rep_tile:
        r0 = t * rows_per_tile
        us = min(max(r0 - NA_WIN_ROWS // 2, 0), n_rows - win_rows)
        r = r0 + np.arange(rows_per_tile)
        rs = np.clip(r - NA_WIN_ROWS // 2, 0, n_rows - NA_WIN_ROWS)
        key_row = us + np.arange(win_rows)
        row_valid = (key_row[None, :] >= rs[:, None]) & (key_row[None, :] < rs[:, None] + NA_WIN_ROWS)
        row_off = key_row[None, :] - r[:, None] + NA_WIN_ROWS - 1
        row_sel.append((row_off[:, :, None] == np.arange(2 * NA_WIN_ROWS - 1)) & row_valid[:, :, None])
        valid.append(row_valid[:, None, :, None] & col_valid[None, :, None, :])
    row_sel = jnp.asarray(np.stack(row_sel), F32)
    valid = jnp.asarray(np.stack(valid))
    toeplitz = jnp.einsum('hrc,wvc->hrwv', rpb.astype(F32), jnp.asarray(col_sel, F32),
                          precision=lax.Precision.HIGHEST)
    bias = jnp.einsum('pijr,hrwv->phiwjv', row_sel, toeplitz, precision=lax.Precision.HIGHEST)
    bias = jnp.where(valid[:, None], bias, NEG_INF)
    bias = bias.reshape(len(rep_tile), heads, rows_per_tile * GRID_W, win_rows * GRID_W)
    return bias, pat_of_tile, win_rows


def neighborhood_attention(qkv, rpb, *, n_lat, n_ctx, rows_per_tile=4):
    heads = rpb.shape[0]
    n_rows = n_lat // GRID_W
    bias, pat_of_tile, win_rows = _na_bias_table(rpb, n_rows, rows_per_tile)
    tq = rows_per_tile * GRID_W
    assert n_ctx == tq, "the context rows must fill exactly one query tile"
    pat_of_tile = np.concatenate([pat_of_tile, np.zeros((1,), np.int32)])
    ctx_blk = n_lat // n_ctx
    grid_spec = pltpu.PrefetchScalarGridSpec(
        num_scalar_prefetch=1,
        grid=(heads, n_rows // rows_per_tile + 1),
        in_specs=[
            pl.BlockSpec((tq, HEAD_DIM), lambda h, t, pat: (t, h)),
            pl.BlockSpec((n_lat, HEAD_DIM), lambda h, t, pat: (0, heads + h)),
            pl.BlockSpec((n_lat, HEAD_DIM), lambda h, t, pat: (0, 2 * heads + h)),
            pl.BlockSpec((n_ctx, HEAD_DIM), lambda h, t, pat: (ctx_blk, heads + h)),
            pl.BlockSpec((n_ctx, HEAD_DIM), lambda h, t, pat: (ctx_blk, 2 * heads + h)),
            pl.BlockSpec((None, None, tq, win_rows * GRID_W), lambda h, t, pat: (pat[t], h, 0, 0)),
        ],
        out_specs=pl.BlockSpec((tq, HEAD_DIM), lambda h, t, pat: (t, h)),
    )
    return pl.pallas_call(
        functools.partial(_na_kernel, rows_per_tile=rows_per_tile, win_rows=win_rows, n_rows=n_rows),
        out_shape=jax.ShapeDtypeStruct((n_lat + n_ctx, heads * HEAD_DIM), BF16),
        grid_spec=grid_spec,
        compiler_params=_cparams(2),
        name="neighborhood_attention",
    )(jnp.asarray(pat_of_tile), qkv, qkv, qkv, qkv, qkv, bias)


CONV_HALO = 16


def _conformer_kernel(ap_ref, gp_ref, a_ref, g_ref, an_ref, gn_ref, cw_ref, cb_ref, lw_ref, lb_ref, o_ref,
                      buf_ref, y_ref, *, tm, first_tiles, last_tiles):
    i = pl.program_id(0)
    ch = a_ref.shape[1]
    has_prev = jnp.logical_not(functools.reduce(jnp.logical_or, [i == t for t in first_tiles]))
    has_next = jnp.logical_not(functools.reduce(jnp.logical_or, [i == t for t in last_tiles]))

    def glu(a, g):
        return a * jax.nn.sigmoid(g)

    buf_ref[0:CONV_HALO, :] = jnp.where(has_prev, glu(ap_ref[...], gp_ref[...]), 0.0)
    buf_ref[CONV_HALO:CONV_HALO + tm, :] = glu(a_ref[...], g_ref[...])
    buf_ref[CONV_HALO + tm:, :] = jnp.where(has_next, glu(an_ref[...], gn_ref[...]), 0.0)

    base = CONV_HALO - CONV_WIDTH // 2

    def chunk(c, carry):
        col = pl.multiple_of(c * LANES, LANES)
        acc = jnp.zeros((tm, LANES), F32) + cb_ref[:, pl.ds(col, LANES)]
        for tap in range(CONV_WIDTH):
            acc = acc + buf_ref[base + tap:base + tap + tm, pl.ds(col, LANES)] * cw_ref[tap:tap + 1, pl.ds(col, LANES)]
        y_ref[:, pl.ds(col, LANES)] = acc
        return carry

    lax.fori_loop(0, ch // LANES, chunk, 0)

    y = y_ref[...]
    mu = jnp.mean(y, axis=-1, keepdims=True)
    yc = y - mu
    z = yc * lax.rsqrt(jnp.mean(yc * yc, axis=-1, keepdims=True) + EPS)
    z = z * lw_ref[...] + lb_ref[...]
    o_ref[...] = (z * jax.nn.sigmoid(z)).astype(o_ref.dtype)


def conformer_conv(ag, cv_w, cv_b, ln_w, ln_b, *, n_lat, tm=256):
    m, two_c = ag.shape
    ch = two_c // 2
    n_tiles = m // tm
    hb = tm // CONV_HALO
    n_hblk = m // CONV_HALO
    first_tiles = (0, n_lat // tm)
    last_tiles = (n_lat // tm - 1, n_tiles - 1)
    prev_map = lambda c: (lambda i: (jnp.maximum(i * hb - 1, 0), c))
    next_map = lambda c: (lambda i: (jnp.minimum((i + 1) * hb, n_hblk - 1), c))
    cur_map = lambda c: (lambda i: (i, c))
    vec = lambda: pl.BlockSpec((1, ch), lambda i: (0, 0))
    return pl.pallas_call(
        functools.partial(_conformer_kernel, tm=tm, first_tiles=first_tiles, last_tiles=last_tiles),
        out_shape=jax.ShapeDtypeStruct((m, ch), BF16),
        grid=(n_tiles,),
        in_specs=[pl.BlockSpec((CONV_HALO, ch), prev_map(0)), pl.BlockSpec((CONV_HALO, ch), prev_map(1)),
                  pl.BlockSpec((tm, ch), cur_map(0)), pl.BlockSpec((tm, ch), cur_map(1)),
                  pl.BlockSpec((CONV_HALO, ch), next_map(0)), pl.BlockSpec((CONV_HALO, ch), next_map(1)),
                  pl.BlockSpec((CONV_WIDTH, ch), lambda i: (0, 0)), vec(), vec(), vec()],
        out_specs=pl.BlockSpec((tm, ch), lambda i: (i, 0)),
        scratch_shapes=[pltpu.VMEM((tm + 2 * CONV_HALO, ch), F32), pltpu.VMEM((tm, ch), F32)],
        compiler_params=_cparams(1),
        name="conformer_conv",
    )(ag, ag, ag, ag, ag, ag, cv_w, cv_b.reshape(1, ch), ln_w.reshape(1, ch), ln_b.reshape(1, ch))


DIFF_Q_TILE = 256
DIFF_KV_CHUNK = 768
LOG2E = 1.4426950408889634
DIFF_SCORE_SCALE = HEAD_DIM ** -0.5 * LOG2E


def _diff_attn_kernel(lam_ref, q_ref, k_ref, vt_ref, sw_ref, o_ref, acc1_ref, acc2_ref, *,
                      key_row0, n_chunks, tk, lambda_init):
    q = (q_ref[:, :HEAD_DIM], q_ref[:, HEAD_DIM:])
    acc_refs = (acc1_ref, acc2_ref)
    tq = q_ref.shape[0]
    m = [jnp.full((1, tq), NEG_INF, F32)] * 2
    l = [jnp.zeros((1, tq), F32)] * 2

    def scores(ci):
        kc = k_ref[key_row0 + ci * tk:key_row0 + (ci + 1) * tk, :]
        return [_dot_nt(kc[:, b * HEAD_DIM:(b + 1) * HEAD_DIM], q[b]) for b in range(2)]

    def softmax_step(s, b):
        m_new = jnp.maximum(m[b], jnp.max(s, axis=0, keepdims=True))
        alpha = jnp.exp2(m[b] - m_new)
        p = jnp.exp2(s - m_new)
        l[b] = alpha * l[b] + jnp.sum(p, axis=0, keepdims=True)
        m[b] = m_new
        return p.astype(BF16), alpha

    def values(ci, p, alpha, b):
        vtc = vt_ref[:, key_row0 + ci * tk:key_row0 + (ci + 1) * tk]
        pv = _dot(vtc, p)
        acc_refs[b][...] = pv if ci == 0 else alpha * acc_refs[b][...] + pv

    s_of, p_of = {}, {}
    for t in range(n_chunks + 2):
        if t < n_chunks:
            s_of[t] = scores(t)
        if 0 <= t - 1 < n_chunks:
            p_of[t - 1] = [softmax_step(s, b) for b, s in enumerate(s_of.pop(t - 1))]
        if 0 <= t - 2 < n_chunks:
            for b, (p, alpha) in enumerate(p_of.pop(t - 2)):
                values(t - 2, p, alpha, b)

    lp = lam_ref[...]
    lam = (jnp.exp(jnp.sum(lp[0:1] * lp[1:2], axis=-1, keepdims=True))
           - jnp.exp(jnp.sum(lp[2:3] * lp[3:4], axis=-1, keepdims=True)) + lambda_init)
    o = acc1_ref[...] * (1.0 / l[0]) - acc2_ref[...] * (lam / l[1])
    o = o * lax.rsqrt(jnp.mean(o * o, axis=0, keepdims=True) + EPS)
    o = (o * sw_ref[...]) * (1.0 - lambda_init)
    o_ref[...] = o.T.astype(o_ref.dtype)


def _kv_chunk(n_keys):
    return max(t for t in range(MXU_DIM, DIFF_KV_CHUNK + 1, MXU_DIM) if n_keys % t == 0)


def diff_attention(qk, vt, lam_p, subln_w, *, heads, q_row0, n_q, key_row0, n_keys, lambda_init, tq):
    m = qk.shape[0]
    hd2 = 2 * HEAD_DIM
    q_blk0 = q_row0 // tq
    tk = _kv_chunk(n_keys)
    scratch = [pltpu.VMEM((hd2, tq), F32), pltpu.VMEM((hd2, tq), F32)]
    return pl.pallas_call(
        functools.partial(_diff_attn_kernel, key_row0=key_row0, n_chunks=n_keys // tk, tk=tk,
                          lambda_init=lambda_init),
        out_shape=jax.ShapeDtypeStruct((n_q, heads * hd2), BF16),
        grid=(heads, n_q // tq),
        in_specs=[pl.BlockSpec((4, HEAD_DIM), lambda h, i: (0, 0)),
                  pl.BlockSpec((tq, hd2), lambda h, i: (q_blk0 + i, h)),
                  pl.BlockSpec((m, hd2), lambda h, i: (0, heads + h)),
                  pl.BlockSpec((hd2, m), lambda h, i: (h, 0)),
                  pl.BlockSpec((hd2, 1), lambda h, i: (0, 0))],
        out_specs=pl.BlockSpec((tq, hd2), lambda h, i: (i, h)),
        scratch_shapes=scratch,
        compiler_params=_cparams(2),
        name="diff_attention",
    )(lam_p.astype(F32), qk, qk, vt, subln_w.reshape(hd2, 1))


FFN_HALO = 16


def _ffn_up_kernel(ap_ref, a_ref, an_ref, wg_ref, wv_ref, cwg_ref, cbg_ref, cwv_ref, cbv_ref, o_ref,
                   aext_ref, *u_refs, tm, seg_starts, seg_ends):
    @pl.when(pl.program_id(1) == 0)
    def _():
        aext_ref[0:FFN_HALO, :] = ap_ref[...]
        aext_ref[FFN_HALO:FFN_HALO + tm, :] = a_ref[...]
        aext_ref[FFN_HALO + tm:, :] = an_ref[...]

    rid = pl.program_id(0) * tm + lax.broadcasted_iota(jnp.int32, (tm, 1), 0)
    has_prev = functools.reduce(jnp.logical_and, [rid != r for r in seg_starts])
    has_next = functools.reduce(jnp.logical_and, [rid != r for r in seg_ends])

    def conv(u_ref, w, b):
        prev = jnp.where(has_prev, u_ref[FFN_HALO - 1:FFN_HALO - 1 + tm, :], 0.0)
        nxt = jnp.where(has_next, u_ref[FFN_HALO + 1:FFN_HALO + 1 + tm, :], 0.0)
        return b + prev * w[0:1, :] + u_ref[FFN_HALO:FFN_HALO + tm, :] * w[1:2, :] + nxt * w[2:3, :]

    a_ext = aext_ref[...]
    n_sub = len(u_refs) // 2
    sub = o_ref.shape[1] // n_sub
    for s in range(n_sub):
        cols = slice(s * sub, (s + 1) * sub)
        ug_ref, uv_ref = u_refs[2 * s], u_refs[2 * s + 1]
        ug_ref[...] = _dot(a_ext, wg_ref[:, cols].astype(BF16))
        uv_ref[...] = _dot(a_ext, wv_ref[:, cols].astype(BF16))
        gate = conv(ug_ref, cwg_ref[:, cols], cbg_ref[:, cols])
        val = conv(uv_ref, cwv_ref[:, cols], cbv_ref[:, cols])
        o_ref[:, cols] = (gate * jax.nn.sigmoid(gate) * val).astype(o_ref.dtype)


def ffn_up_conv_gate(a, w_up, layer, conv_w, conv_b, *, n_lat):
    m, k = a.shape
    two_f = w_up.shape[2]
    f = two_f // 2
    tm, tn = _pick_tiles(m, f, lambda tn: 2 * 2 * k * tn * 4 + 2 * k * MXU_DIM * 2,
                         lambda tn: k * 2 + k * 2 + 2 * tn * 4 + 2 * tn * 2, col_tiles=COL_TILES[:1])
    hb = tm // FFN_HALO
    n_hblk = m // FFN_HALO
    voff = f // tn
    n_sub = tn // MXU_DIM
    b2 = conv_b.reshape(1, two_f)
    u_scratch = [pltpu.VMEM((tm + 2 * FFN_HALO, tn // n_sub), F32) for _ in range(2 * n_sub)]
    return pl.pallas_call(
        functools.partial(_ffn_up_kernel, tm=tm, seg_starts=(0, n_lat), seg_ends=(n_lat - 1, m - 1)),
        out_shape=jax.ShapeDtypeStruct((m, f), BF16),
        grid=(m // tm, f // tn),
        in_specs=[pl.BlockSpec((FFN_HALO, k), lambda i, j: (jnp.maximum(i * hb - 1, 0), 0)),
                  pl.BlockSpec((tm, k), lambda i, j: (i, 0), pipeline_mode=pl.Buffered(1)),
                  pl.BlockSpec((FFN_HALO, k), lambda i, j: (jnp.minimum((i + 1) * hb, n_hblk - 1), 0)),
                  pl.BlockSpec((None, k, tn), lambda i, j: (layer, 0, j)),
                  pl.BlockSpec((None, k, tn), lambda i, j: (layer, 0, j + voff)),
                  pl.BlockSpec((FFN_CONV_WIDTH, tn), lambda i, j: (0, j)),
                  pl.BlockSpec((1, tn), lambda i, j: (0, j)),
                  pl.BlockSpec((FFN_CONV_WIDTH, tn), lambda i, j: (0, j + voff)),
                  pl.BlockSpec((1, tn), lambda i, j: (0, j + voff))],
        out_specs=pl.BlockSpec((tm, tn), lambda i, j: (i, j)),
        scratch_shapes=[pltpu.VMEM((tm + 2 * FFN_HALO, k), BF16)] + u_scratch,
        compiler_params=_cparams(2),
        name="ffn_up_conv_gate",
    )(a, a, a, w_up, w_up, conv_w, b2, conv_w, b2)


def _rope_tables(n_lat, n_ctx):
    t = jnp.arange(n_lat)
    row = (t // GRID_W).astype(F32)
    col = (t % GRID_W).astype(F32)
    per_axis = HEAD_DIM // 2
    inv = ROPE_BASE ** (-jnp.arange(0, per_axis, 2, dtype=F32) / per_axis)
    ang_r, ang_c = row[:, None] * inv, col[:, None] * inv
    cr, sr, cc, sc = jnp.cos(ang_r), jnp.sin(ang_r), jnp.cos(ang_c), jnp.sin(ang_c)
    zero = jnp.zeros_like(sr)
    cos = jnp.concatenate([cr, cr, cc, cc], axis=1)
    sin_a = jnp.concatenate([-sr, zero, -sc, zero], axis=1)
    sin_b = jnp.concatenate([zero, sr, zero, sc], axis=1)
    pad = lambda x, v: jnp.concatenate([x, jnp.full((n_ctx, HEAD_DIM), v, F32)], axis=0)
    return pad(cos, 1.0), pad(sin_a, 0.0), pad(sin_b, 0.0)


def kernel(x, c, ctx, c_ctx, mod_w, mod_b, norm1_w, norm2_w, na_w_in, na_w_out, na_rpb, cv_w, cv_b, cv_ln_w,
           cv_ln_b, diff_w_in, diff_w_out, diff_lambda, diff_subln_w, ffn_w_up, ffn_conv_w, ffn_conv_b,
           ffn_w_down, final_norm_w):
    assert x.shape[0] == 1 and ctx.shape[0] == 1
    n_lat, d = x.shape[1], x.shape[2]
    n_ctx = ctx.shape[1]
    depth = mod_w.shape[0]
    m = n_lat + n_ctx
    na_width = na_rpb.shape[1] * HEAD_DIM
    diff_heads = diff_w_in.shape[2] // (3 * 2 * HEAD_DIM)
    diff_width = diff_heads * 2 * HEAD_DIM

    h = jnp.concatenate([x[0], ctx[0]], axis=0)
    mod = adaln_modulation(jnp.stack([c[0], c_ctx], axis=1), mod_w, mod_b)
    mod = mod.reshape(depth, 2, 6, d)
    cos, sin_a, sin_b = _rope_tables(n_lat, n_ctx)

    for l in range(depth):
        i = l // 2
        shift1, scale1, gate1, shift2, scale2, gate2 = (mod[l, :, j, :] for j in range(6))
        a = norm_modulate(h, norm1_w[l], shift1, scale1, n_lat=n_lat, out_dtype=BF16)
        if l % 2 == 0:
            qkv = matmul(a, na_w_in, i, col0=0, n_cols=3 * na_width, out_dtype=BF16)
            ag = matmul(a, na_w_in, i, col0=3 * na_width, n_cols=na_w_in.shape[2] - 3 * na_width, out_dtype=F32)
            o_na = neighborhood_attention(qkv, na_rpb[i], n_lat=n_lat, n_ctx=n_ctx)
            o_cv = conformer_conv(ag, cv_w[i], cv_b[i], cv_ln_w[i], cv_ln_b[i], n_lat=n_lat)
            h = matmul_residual([o_na, o_cv], na_w_out, i, h, gate1, n_lat=n_lat)
        else:
            lambda_init = 0.8 - 0.6 * math.exp(-0.3 * l)
            qk = matmul_rope(a, diff_w_in, i, cos, sin_a, sin_b, n_cols=2 * diff_width, scaled_cols=diff_width,
                             scale=DIFF_SCORE_SCALE, out_dtype=BF16)
            w_v_t = diff_w_in[i][:, 2 * diff_width:].T.astype(BF16)
            vt = matmul_transposed_out(w_v_t, a, out_dtype=BF16)
            o_lat = diff_attention(qk, vt, diff_lambda[i], diff_subln_w[i], heads=diff_heads, q_row0=0, n_q=n_lat,
                                   key_row0=0, n_keys=m, lambda_init=lambda_init, tq=min(DIFF_Q_TILE, n_lat))
            o_ctx = diff_attention(qk, vt, diff_lambda[i], diff_subln_w[i], heads=diff_heads, q_row0=n_lat,
                                   n_q=n_ctx, key_row0=n_lat, n_keys=n_ctx, lambda_init=lambda_init, tq=n_ctx)
            o = jnp.concatenate([o_lat, o_ctx], axis=0)
            h = matmul_residual([o], diff_w_out, i, h, gate1, n_lat=n_lat)
        f = norm_modulate(h, norm2_w[l], shift2, scale2, n_lat=n_lat, out_dtype=BF16)
        act = ffn_up_conv_gate(f, ffn_w_up, l, ffn_conv_w[l], ffn_conv_b[l], n_lat=n_lat)
        h = matmul_residual([act], ffn_w_down, l, h, gate2, n_lat=n_lat)

    zeros2 = jnp.zeros((2, d), F32)
    out = norm_modulate(h, final_norm_w, zeros2, zeros2, n_lat=n_lat, out_dtype=F32, n_rows=n_lat)
    return out[None]
```

```python
import functools
import math

import jax
import jax.numpy as jnp
import numpy as np
from jax import lax
from jax.experimental import pallas as pl
from jax.experimental.pallas import tpu as pltpu

GRID_W = 64
HEAD_DIM = 128
NA_WIN_ROWS = 8
NA_WIN_COLS = 16
CONV_WIDTH = 31
FFN_CONV_WIDTH = 3
ROPE_BASE = 10000.0
EPS = 1e-6

V7X_VMEM_BYTES = 64 * 1024 * 1024
VMEM_LIMIT = V7X_VMEM_BYTES * 7 // 8
LANES = 128
SUBLANES = 8
NEG_INF = -1e30

BF16 = jnp.bfloat16
F32 = jnp.float32


def _cparams(n_grid_axes):
    return pltpu.CompilerParams(dimension_semantics=("arbitrary",) * n_grid_axes,
                                vmem_limit_bytes=VMEM_LIMIT)


def _dot(a, b):
    return jnp.dot(a, b, preferred_element_type=F32)


def _dot_nt(a, b):
    return lax.dot_general(a, b, (((1,), (1,)), ((), ())), preferred_element_type=F32)


def _seg_select(row0, n_rows, n_lat, two_rows):
    rid = row0 + lax.broadcasted_iota(jnp.int32, (n_rows, 1), 0)
    return jnp.where(rid >= n_lat, two_rows[1:2, :], two_rows[0:1, :])


def _mod_kernel(c_ref, w_ref, b_ref, o_ref, s_ref, *, tn):
    k_dim = w_ref.shape[0]

    @pl.when((pl.program_id(0) == 0) & (pl.program_id(1) == 0))
    def _():
        x = c_ref[...]
        s = x * jax.nn.sigmoid(x)
        s_ref[0] = jnp.broadcast_to(s[:, 0:1], (k_dim, LANES))
        s_ref[1] = jnp.broadcast_to(s[:, 1:2], (k_dim, LANES))

    n_groups = tn // LANES

    def body(k, accs):
        r = pl.multiple_of(k * SUBLANES, SUBLANES)
        s0 = s_ref[0, pl.ds(r, SUBLANES), :]
        s1 = s_ref[1, pl.ds(r, SUBLANES), :]
        out = []
        for g in range(n_groups):
            w = w_ref[pl.ds(r, SUBLANES), g * LANES:(g + 1) * LANES]
            out.append(accs[2 * g] + s0 * w)
            out.append(accs[2 * g + 1] + s1 * w)
        return tuple(out)

    zeros = tuple(jnp.zeros((SUBLANES, LANES), F32) for _ in range(2 * n_groups))
    accs = lax.fori_loop(0, k_dim // SUBLANES, body, zeros, unroll=8)
    for g in range(n_groups):
        b = b_ref[:, g * LANES:(g + 1) * LANES]
        o_ref[0:1, g * LANES:(g + 1) * LANES] = jnp.sum(accs[2 * g], axis=0, keepdims=True) + b
        o_ref[1:2, g * LANES:(g + 1) * LANES] = jnp.sum(accs[2 * g + 1], axis=0, keepdims=True) + b


def adaln_modulation(c2, mod_w, mod_b, *, tn=512):
    n_layers, k_dim, n = mod_w.shape
    tn = min(tn, n)
    return pl.pallas_call(
        functools.partial(_mod_kernel, tn=tn),
        out_shape=jax.ShapeDtypeStruct((n_layers, 2, n), F32),
        grid=(n_layers, n // tn),
        in_specs=[
            pl.BlockSpec((k_dim, 2), lambda l, j: (0, 0)),
            pl.BlockSpec((None, k_dim, tn), lambda l, j: (l, 0, j)),
            pl.BlockSpec((None, 1, tn), lambda l, j: (l, 0, j)),
        ],
        out_specs=pl.BlockSpec((None, 2, tn), lambda l, j: (l, 0, j)),
        scratch_shapes=[pltpu.VMEM((2, k_dim, LANES), F32)],
        compiler_params=_cparams(2),
        name="adaln_mod",
    )(c2, mod_w, mod_b.reshape(n_layers, 1, n))


def _norm_mod_kernel(h_ref, w_ref, shift_ref, scale_ref, o_ref):
    x = h_ref[...]
    gain = w_ref[...] * (1.0 + scale_ref[...])
    y = x * lax.rsqrt(jnp.mean(x * x, axis=-1, keepdims=True) + EPS)
    o_ref[...] = (y * gain + shift_ref[...]).astype(o_ref.dtype)


def norm_modulate(h, w, shift2, scale2, *, n_lat, out_dtype, n_rows=None, tm=256):
    m, d = h.shape
    m = m if n_rows is None else n_rows
    tm = min(tm, m)
    assert n_lat % tm == 0, "row tiles must not straddle the latent/context boundary"
    lat_tiles = n_lat // tm
    seg_row = lambda i: (jnp.where(i >= lat_tiles, 1, 0), 0, 0)
    return pl.pallas_call(
        _norm_mod_kernel,
        out_shape=jax.ShapeDtypeStruct((m, d), out_dtype),
        grid=(m // tm,),
        in_specs=[
            pl.BlockSpec((tm, d), lambda i: (i, 0)),
            pl.BlockSpec((1, d), lambda i: (0, 0)),
            pl.BlockSpec((None, 1, d), seg_row),
            pl.BlockSpec((None, 1, d), seg_row),
        ],
        out_specs=pl.BlockSpec((tm, d), lambda i: (i, 0)),
        compiler_params=_cparams(1),
        name="norm_modulate",
    )(h, w.reshape(1, d), shift2.reshape(2, 1, d), scale2.reshape(2, 1, d))


MXU_DIM = 256
ROW_ALIGN = 128
COL_TILES = (512, 256)


def _pick_tiles(m, n, fixed_bytes, row_bytes, col_tiles=COL_TILES):
    best = None
    for tn in col_tiles:
        if n % tn:
            continue
        fits = [t for t in range(ROW_ALIGN, m + 1, ROW_ALIGN)
                if m % t == 0 and fixed_bytes(tn) + t * row_bytes(tn) <= VMEM_LIMIT]
        if fits and (best is None or max(fits) > best[0]):
            best = (max(fits), tn)
    assert best is not None, "no tile fits the VMEM limit"
    return best


def _f32_weight_tile_bytes(k, tn):
    return 2 * k * tn * 4 + k * tn * 2


def _mm_kernel(a_ref, w_ref, o_ref):
    o_ref[...] = _dot(a_ref[...], w_ref[...].astype(BF16)).astype(o_ref.dtype)


def matmul(a, w, layer, *, col0, n_cols, out_dtype):
    m, k = a.shape
    ob = jnp.dtype(out_dtype).itemsize
    tm, tn = _pick_tiles(m, math.gcd(n_cols, col0) if col0 else n_cols,
                         lambda tn: _f32_weight_tile_bytes(k, tn),
                         lambda tn: 2 * k * 2 + 2 * tn * ob + tn * 4)
    blk0 = col0 // tn
    return pl.pallas_call(
        _mm_kernel,
        out_shape=jax.ShapeDtypeStruct((m, n_cols), out_dtype),
        grid=(m // tm, n_cols // tn),
        in_specs=[pl.BlockSpec((tm, k), lambda i, j: (i, 0)),
                  pl.BlockSpec((None, k, tn), lambda i, j: (layer, 0, blk0 + j))],
        out_specs=pl.BlockSpec((tm, tn), lambda i, j: (i, j)),
        compiler_params=_cparams(2),
        name="matmul",
    )(a, w)


def _mm_t_kernel(a_ref, w_ref, o_ref):
    o_ref[...] = _dot(a_ref[...], w_ref[...].astype(BF16)).T.astype(o_ref.dtype)


def matmul_transposed_out(a, w, layer, *, col0, n_cols, out_dtype):
    m, k = a.shape
    ob = jnp.dtype(out_dtype).itemsize
    tm, tn = _pick_tiles(m, math.gcd(n_cols, col0) if col0 else n_cols,
                         lambda tn: _f32_weight_tile_bytes(k, tn),
                         lambda tn: 2 * k * 2 + 2 * tn * ob + 2 * tn * 4)
    blk0 = col0 // tn
    return pl.pallas_call(
        _mm_t_kernel,
        out_shape=jax.ShapeDtypeStruct((n_cols, m), out_dtype),
        grid=(m // tm, n_cols // tn),
        in_specs=[pl.BlockSpec((tm, k), lambda i, j: (i, 0)),
                  pl.BlockSpec((None, k, tn), lambda i, j: (layer, 0, blk0 + j))],
        out_specs=pl.BlockSpec((tn, tm), lambda i, j: (j, i)),
        compiler_params=_cparams(2),
        name="matmul_transposed_out",
    )(a, w)


def _mm_rope_kernel(a_ref, w_ref, cos_ref, sa_ref, sb_ref, o_ref, *, tn, scaled_tiles, scale):
    acc = _dot(a_ref[...], w_ref[...].astype(BF16))
    reps = tn // HEAD_DIM
    cos = jnp.concatenate([cos_ref[...]] * reps, axis=1)
    sa = jnp.concatenate([sa_ref[...]] * reps, axis=1)
    sb = jnp.concatenate([sb_ref[...]] * reps, axis=1)
    up = pltpu.roll(acc, tn - HEAD_DIM // 4, 1)
    dn = pltpu.roll(acc, HEAD_DIM // 4, 1)
    out = acc * cos + up * sa + dn * sb
    out = out * jnp.where(pl.program_id(1) < scaled_tiles, scale, 1.0)
    o_ref[...] = out.astype(o_ref.dtype)


def matmul_rope(a, w, layer, cos, sin_a, sin_b, *, n_cols, scaled_cols, scale, out_dtype):
    m, k = a.shape
    ob = jnp.dtype(out_dtype).itemsize
    tm, tn = _pick_tiles(m, math.gcd(n_cols, scaled_cols),
                         lambda tn: _f32_weight_tile_bytes(k, tn),
                         lambda tn: 2 * k * 2 + 2 * tn * ob + 3 * 2 * HEAD_DIM * 4 + 4 * tn * 4)
    return pl.pallas_call(
        functools.partial(_mm_rope_kernel, tn=tn, scaled_tiles=scaled_cols // tn, scale=scale),
        out_shape=jax.ShapeDtypeStruct((m, n_cols), out_dtype),
        grid=(m // tm, n_cols // tn),
        in_specs=[pl.BlockSpec((tm, k), lambda i, j: (i, 0)),
                  pl.BlockSpec((None, k, tn), lambda i, j: (layer, 0, j)),
                  pl.BlockSpec((tm, HEAD_DIM), lambda i, j: (i, 0)),
                  pl.BlockSpec((tm, HEAD_DIM), lambda i, j: (i, 0)),
                  pl.BlockSpec((tm, HEAD_DIM), lambda i, j: (i, 0))],
        out_specs=pl.BlockSpec((tm, tn), lambda i, j: (i, j)),
        compiler_params=_cparams(2),
        name="matmul_rope",
    )(a, w, cos, sin_a, sin_b)


def _mm_res_kernel(*refs, n_a, n_lat, tm):
    a_refs = refs[:n_a]
    w_ref, res_ref, gate_ref, o_ref = refs[n_a:]
    acc = None
    k0 = 0
    for a_ref in a_refs:
        kk = a_ref.shape[1]
        part = _dot(a_ref[...], w_ref[k0:k0 + kk, :].astype(BF16))
        acc = part if acc is None else acc + part
        k0 += kk
    gate = _seg_select(pl.program_id(0) * tm, tm, n_lat, gate_ref[...])
    o_ref[...] = res_ref[...] + gate * acc


def matmul_residual(a_list, w, layer, res, gate2, *, n_lat):
    m = res.shape[0]
    _, k, n = w.shape
    tm, tn = _pick_tiles(m, n, lambda tn: _f32_weight_tile_bytes(k, tn),
                         lambda tn: 2 * k * 2 + 2 * tn * 4 + 2 * tn * 4 + tn * 4)
    in_specs = [pl.BlockSpec((tm, a.shape[1]), lambda i, j: (i, 0)) for a in a_list]
    in_specs += [pl.BlockSpec((None, k, tn), lambda i, j: (layer, 0, j)),
                 pl.BlockSpec((tm, tn), lambda i, j: (i, j)),
                 pl.BlockSpec((2, tn), lambda i, j: (0, j))]
    return pl.pallas_call(
        functools.partial(_mm_res_kernel, n_a=len(a_list), n_lat=n_lat, tm=tm),
        out_shape=jax.ShapeDtypeStruct((m, n), F32),
        grid=(m // tm, n // tn),
        in_specs=in_specs,
        out_specs=pl.BlockSpec((tm, tn), lambda i, j: (i, j)),
        compiler_params=_cparams(2),
        name="matmul_residual",
    )(*a_list, w, res, gate2)


NA_HEADS_PER_STEP = 2


def _na_window_start(t, rows_per_tile, win_rows, n_rows):
    return jnp.clip(t * rows_per_tile - NA_WIN_ROWS // 2, 0, n_rows - win_rows)


def _na_kernel(pat_ref, q_ref, k_ref, v_ref, kc_ref, vc_ref, bias_ref, o_ref, *,
               rows_per_tile, win_rows, n_rows):
    del pat_ref
    t = pl.program_id(1)
    scale = HEAD_DIM ** -0.5
    start = pl.multiple_of(_na_window_start(t, rows_per_tile, win_rows, n_rows) * GRID_W, GRID_W)
    is_ctx_tile = t == n_rows // rows_per_tile
    for hh in range(NA_HEADS_PER_STEP):
        lanes = slice(hh * HEAD_DIM, (hh + 1) * HEAD_DIM)
        q = q_ref[:, lanes]
        k_loc = k_ref[pl.ds(start, win_rows * GRID_W), lanes]
        v_loc = v_ref[pl.ds(start, win_rows * GRID_W), lanes]
        bias = jnp.where(is_ctx_tile, NEG_INF, bias_ref[hh])
        s_loc = _dot_nt(q, k_loc) * scale + bias
        s_ctx = _dot_nt(q, kc_ref[:, lanes]) * scale
        m = jnp.maximum(jnp.max(s_loc, axis=-1, keepdims=True), jnp.max(s_ctx, axis=-1, keepdims=True))
        p_loc = jnp.exp(s_loc - m)
        p_ctx = jnp.exp(s_ctx - m)
        denom = jnp.sum(p_loc, axis=-1, keepdims=True) + jnp.sum(p_ctx, axis=-1, keepdims=True)
        o = _dot(p_loc.astype(BF16), v_loc) + _dot(p_ctx.astype(BF16), vc_ref[:, lanes])
        o_ref[:, lanes] = (o / denom).astype(o_ref.dtype)


def _na_tile_patterns(n_rows, rows_per_tile):
    win_rows = rows_per_tile + NA_WIN_ROWS - 1
    keys, pat_of_tile, rep_tile = {}, [], []
    for t in range(n_rows // rows_per_tile):
        r0 = t * rows_per_tile
        us = min(max(r0 - NA_WIN_ROWS // 2, 0), n_rows - win_rows)
        rs = tuple(min(max(r0 + i - NA_WIN_ROWS // 2, 0), n_rows - NA_WIN_ROWS) - us
                   for i in range(rows_per_tile))
        key = (r0 - us, rs)
        if key not in keys:
            keys[key] = len(keys)
            rep_tile.append(t)
        pat_of_tile.append(keys[key])
    return np.asarray(pat_of_tile, np.int32), rep_tile, win_rows


def _na_bias_table(rpb, n_rows, rows_per_tile):
    pat_of_tile, rep_tile, win_rows = _na_tile_patterns(n_rows, rows_per_tile)
    heads = rpb.shape[0]
    w = np.arange(GRID_W)
    col_start = np.clip(w - NA_WIN_COLS // 2, 0, GRID_W - NA_WIN_COLS)
    col_valid = (w[None, :] >= col_start[:, None]) & (w[None, :] < col_start[:, None] + NA_WIN_COLS)
    col_off = w[None, :] - w[:, None] + NA_WIN_COLS - 1
    col_sel = (col_off[:, :, None] == np.arange(2 * NA_WIN_COLS - 1)) & col_valid[:, :, None]
    row_sel, valid = [], []
    for t in rep_tile:
        r0 = t * rows_per_tile
        us = min(max(r0 - NA_WIN_ROWS // 2, 0), n_rows - win_rows)
        r = r0 + np.arange(rows_per_tile)
        rs = np.clip(r - NA_WIN_ROWS // 2, 0, n_rows - NA_WIN_ROWS)
        key_row = us + np.arange(win_rows)
        row_valid = (key_row[None, :] >= rs[:, None]) & (key_row[None, :] < rs[:, None] + NA_WIN_ROWS)
        row_off = key_row[None, :] - r[:, None] + NA_WIN_ROWS - 1
        row_sel.append((row_off[:, :, None] == np.arange(2 * NA_WIN_ROWS - 1)) & row_valid[:, :, None])
        valid.append(row_valid[:, None, :, None] & col_valid[None, :, None, :])
    row_sel = jnp.asarray(np.stack(row_sel), F32)
    valid = jnp.asarray(np.stack(valid))
    toeplitz = jnp.einsum('hrc,wvc->hrwv', rpb.astype(F32), jnp.asarray(col_sel, F32),
                          precision=lax.Precision.HIGHEST)
    bias = jnp.einsum('pijr,hrwv->phiwjv', row_sel, toeplitz, precision=lax.Precision.HIGHEST)
    bias = jnp.where(valid[:, None], bias, NEG_INF)
    bias = bias.reshape(len(rep_tile), heads, rows_per_tile * GRID_W, win_rows * GRID_W)
    return bias, pat_of_tile, win_rows


def neighborhood_attention(qkv, rpb, *, n_lat, n_ctx, rows_per_tile=4):
    heads = rpb.shape[0]
    n_rows = n_lat // GRID_W
    bias, pat_of_tile, win_rows = _na_bias_table(rpb, n_rows, rows_per_tile)
    tq = rows_per_tile * GRID_W
    assert n_ctx == tq, "the context rows must fill exactly one query tile"
    pat_of_tile = np.concatenate([pat_of_tile, np.zeros((1,), np.int32)])
    ctx_blk = n_lat // n_ctx
    hps = NA_HEADS_PER_STEP
    groups = heads // hps
    wide = hps * HEAD_DIM
    grid_spec = pltpu.PrefetchScalarGridSpec(
        num_scalar_prefetch=1,
        grid=(groups, n_rows // rows_per_tile + 1),
        in_specs=[
            pl.BlockSpec((tq, wide), lambda g, t, pat: (t, g)),
            pl.BlockSpec((n_lat, wide), lambda g, t, pat: (0, groups + g)),
            pl.BlockSpec((n_lat, wide), lambda g, t, pat: (0, 2 * groups + g)),
            pl.BlockSpec((n_ctx, wide), lambda g, t, pat: (ctx_blk, groups + g)),
            pl.BlockSpec((n_ctx, wide), lambda g, t, pat: (ctx_blk, 2 * groups + g)),
            pl.BlockSpec((None, hps, tq, win_rows * GRID_W), lambda g, t, pat: (pat[t], g, 0, 0)),
        ],
        out_specs=pl.BlockSpec((tq, wide), lambda g, t, pat: (t, g)),
    )
    return pl.pallas_call(
        functools.partial(_na_kernel, rows_per_tile=rows_per_tile, win_rows=win_rows, n_rows=n_rows),
        out_shape=jax.ShapeDtypeStruct((n_lat + n_ctx, heads * HEAD_DIM), BF16),
        grid_spec=grid_spec,
        compiler_params=_cparams(2),
        name="neighborhood_attention",
    )(jnp.asarray(pat_of_tile), qkv, qkv, qkv, qkv, qkv, bias)


CONV_HALO = 16


def _conformer_kernel(ap_ref, gp_ref, a_ref, g_ref, an_ref, gn_ref, cw_ref, cb_ref, lw_ref, lb_ref, o_ref,
                      buf_ref, y_ref, *, tm, first_tiles, last_tiles):
    i = pl.program_id(0)
    ch = a_ref.shape[1]
    has_prev = jnp.logical_not(functools.reduce(jnp.logical_or, [i == t for t in first_tiles]))
    has_next = jnp.logical_not(functools.reduce(jnp.logical_or, [i == t for t in last_tiles]))

    def glu(a, g):
        return a * jax.nn.sigmoid(g)

    buf_ref[0:CONV_HALO, :] = jnp.where(has_prev, glu(ap_ref[...], gp_ref[...]), 0.0)
    buf_ref[CONV_HALO:CONV_HALO + tm, :] = glu(a_ref[...], g_ref[...])
    buf_ref[CONV_HALO + tm:, :] = jnp.where(has_next, glu(an_ref[...], gn_ref[...]), 0.0)

    base = CONV_HALO - CONV_WIDTH // 2

    def chunk(c, carry):
        col = pl.multiple_of(c * LANES, LANES)
        acc = jnp.zeros((tm, LANES), F32) + cb_ref[:, pl.ds(col, LANES)]
        for tap in range(CONV_WIDTH):
            acc = acc + buf_ref[base + tap:base + tap + tm, pl.ds(col, LANES)] * cw_ref[tap:tap + 1, pl.ds(col, LANES)]
        y_ref[:, pl.ds(col, LANES)] = acc
        return carry

    lax.fori_loop(0, ch // LANES, chunk, 0)

    y = y_ref[...]
    mu = jnp.mean(y, axis=-1, keepdims=True)
    yc = y - mu
    z = yc * lax.rsqrt(jnp.mean(yc * yc, axis=-1, keepdims=True) + EPS)
    z = z * lw_ref[...] + lb_ref[...]
    o_ref[...] = (z * jax.nn.sigmoid(z)).astype(o_ref.dtype)


def conformer_conv(ag, cv_w, cv_b, ln_w, ln_b, *, n_lat, tm=256):
    m, two_c = ag.shape
    ch = two_c // 2
    n_tiles = m // tm
    hb = tm // CONV_HALO
    n_hblk = m // CONV_HALO
    first_tiles = (0, n_lat // tm)
    last_tiles = (n_lat // tm - 1, n_tiles - 1)
    prev_map = lambda c: (lambda i: (jnp.maximum(i * hb - 1, 0), c))
    next_map = lambda c: (lambda i: (jnp.minimum((i + 1) * hb, n_hblk - 1), c))
    cur_map = lambda c: (lambda i: (i, c))
    vec = lambda: pl.BlockSpec((1, ch), lambda i: (0, 0))
    return pl.pallas_call(
        functools.partial(_conformer_kernel, tm=tm, first_tiles=first_tiles, last_tiles=last_tiles),
        out_shape=jax.ShapeDtypeStruct((m, ch), BF16),
        grid=(n_tiles,),
        in_specs=[pl.BlockSpec((CONV_HALO, ch), prev_map(0)), pl.BlockSpec((CONV_HALO, ch), prev_map(1)),
                  pl.BlockSpec((tm, ch), cur_map(0)), pl.BlockSpec((tm, ch), cur_map(1)),
                  pl.BlockSpec((CONV_HALO, ch), next_map(0)), pl.BlockSpec((CONV_HALO, ch), next_map(1)),
                  pl.BlockSpec((CONV_WIDTH, ch), lambda i: (0, 0)), vec(), vec(), vec()],
        out_specs=pl.BlockSpec((tm, ch), lambda i: (i, 0)),
        scratch_shapes=[pltpu.VMEM((tm + 2 * CONV_HALO, ch), F32), pltpu.VMEM((tm, ch), F32)],
        compiler_params=_cparams(1),
        name="conformer_conv",
    )(ag, ag, ag, ag, ag, ag, cv_w, cv_b.reshape(1, ch), ln_w.reshape(1, ch), ln_b.reshape(1, ch))


DIFF_Q_TILE = 256
DIFF_KV_CHUNK = 768
LOG2E = 1.4426950408889634
DIFF_SCORE_SCALE = HEAD_DIM ** -0.5 * LOG2E


def _diff_attn_kernel(lam_ref, q_ref, k_ref, vt_ref, sw_ref, o_ref, acc1_ref, acc2_ref, *,
                      key_row0, n_chunks, tk, lambda_init):
    q = (q_ref[:, :HEAD_DIM], q_ref[:, HEAD_DIM:])
    acc_refs = (acc1_ref, acc2_ref)
    tq = q_ref.shape[0]
    m = [jnp.full((1, tq), NEG_INF, F32)] * 2
    l = [jnp.zeros((1, tq), F32)] * 2

    def scores(ci):
        kc = k_ref[key_row0 + ci * tk:key_row0 + (ci + 1) * tk, :]
        return [_dot_nt(kc[:, b * HEAD_DIM:(b + 1) * HEAD_DIM], q[b]) for b in range(2)]

    def softmax_step(s, b):
        m_new = jnp.maximum(m[b], jnp.max(s, axis=0, keepdims=True))
        alpha = jnp.exp2(m[b] - m_new)
        p = jnp.exp2(s - m_new)
        l[b] = alpha * l[b] + jnp.sum(p, axis=0, keepdims=True)
        m[b] = m_new
        return p.astype(BF16), alpha

    def values(ci, p, alpha, b):
        vtc = vt_ref[:, key_row0 + ci * tk:key_row0 + (ci + 1) * tk]
        pv = _dot(vtc, p)
        acc_refs[b][...] = pv if ci == 0 else alpha * acc_refs[b][...] + pv

    s_of, p_of = {}, {}
    for t in range(n_chunks + 2):
        if t < n_chunks:
            s_of[t] = scores(t)
        if 0 <= t - 1 < n_chunks:
            p_of[t - 1] = [softmax_step(s, b) for b, s in enumerate(s_of.pop(t - 1))]
        if 0 <= t - 2 < n_chunks:
            for b, (p, alpha) in enumerate(p_of.pop(t - 2)):
                values(t - 2, p, alpha, b)

    lp = lam_ref[...]
    lam = (jnp.exp(jnp.sum(lp[0:1] * lp[1:2], axis=-1, keepdims=True))
           - jnp.exp(jnp.sum(lp[2:3] * lp[3:4], axis=-1, keepdims=True)) + lambda_init)
    o = acc1_ref[...] * (1.0 / l[0]) - acc2_ref[...] * (lam / l[1])
    o = o * lax.rsqrt(jnp.mean(o * o, axis=0, keepdims=True) + EPS)
    o = (o * sw_ref[...]) * (1.0 - lambda_init)
    o_ref[...] = o.T.astype(o_ref.dtype)


def _kv_chunk(n_keys):
    return max(t for t in range(MXU_DIM, DIFF_KV_CHUNK + 1, MXU_DIM) if n_keys % t == 0)


def diff_attention(qk, vt, lam_p, subln_w, *, heads, q_row0, n_q, key_row0, n_keys, lambda_init, tq):
    m = qk.shape[0]
    hd2 = 2 * HEAD_DIM
    q_blk0 = q_row0 // tq
    tk = _kv_chunk(n_keys)
    scratch = [pltpu.VMEM((hd2, tq), F32), pltpu.VMEM((hd2, tq), F32)]
    return pl.pallas_call(
        functools.partial(_diff_attn_kernel, key_row0=key_row0, n_chunks=n_keys // tk, tk=tk,
                          lambda_init=lambda_init),
        out_shape=jax.ShapeDtypeStruct((n_q, heads * hd2), BF16),
        grid=(heads, n_q // tq),
        in_specs=[pl.BlockSpec((4, HEAD_DIM), lambda h, i: (0, 0)),
                  pl.BlockSpec((tq, hd2), lambda h, i: (q_blk0 + i, h)),
                  pl.BlockSpec((m, hd2), lambda h, i: (0, heads + h)),
                  pl.BlockSpec((hd2, m), lambda h, i: (h, 0)),
                  pl.BlockSpec((hd2, 1), lambda h, i: (0, 0))],
        out_specs=pl.BlockSpec((tq, hd2), lambda h, i: (i, h)),
        scratch_shapes=scratch,
        compiler_params=_cparams(2),
        name="diff_attention",
    )(lam_p.astype(F32), qk, qk, vt, subln_w.reshape(hd2, 1))


FFN_HALO = 16


def _ffn_up_kernel(ap_ref, a_ref, an_ref, wg_ref, wv_ref, cwg_ref, cbg_ref, cwv_ref, cbv_ref, o_ref,
                   aext_ref, *u_refs, tm, seg_starts, seg_ends):
    @pl.when(pl.program_id(1) == 0)
    def _():
        aext_ref[0:FFN_HALO, :] = ap_ref[...]
        aext_ref[FFN_HALO:FFN_HALO + tm, :] = a_ref[...]
        aext_ref[FFN_HALO + tm:, :] = an_ref[...]

    rid = pl.program_id(0) * tm + lax.broadcasted_iota(jnp.int32, (tm, 1), 0)
    has_prev = functools.reduce(jnp.logical_and, [rid != r for r in seg_starts])
    has_next = functools.reduce(jnp.logical_and, [rid != r for r in seg_ends])

    def conv(u_ref, w, b):
        prev = jnp.where(has_prev, u_ref[FFN_HALO - 1:FFN_HALO - 1 + tm, :], 0.0)
        nxt = jnp.where(has_next, u_ref[FFN_HALO + 1:FFN_HALO + 1 + tm, :], 0.0)
        return b + prev * w[0:1, :] + u_ref[FFN_HALO:FFN_HALO + tm, :] * w[1:2, :] + nxt * w[2:3, :]

    a_ext = aext_ref[...]
    n_sub = len(u_refs) // 2
    sub = o_ref.shape[1] // n_sub
    for s in range(n_sub):
        cols = slice(s * sub, (s + 1) * sub)
        ug_ref, uv_ref = u_refs[2 * s], u_refs[2 * s + 1]
        ug_ref[...] = _dot(a_ext, wg_ref[:, cols])
        uv_ref[...] = _dot(a_ext, wv_ref[:, cols])
        gate = conv(ug_ref, cwg_ref[:, cols], cbg_ref[:, cols])
        val = conv(uv_ref, cwv_ref[:, cols], cbv_ref[:, cols])
        o_ref[:, cols] = (gate * jax.nn.sigmoid(gate) * val).astype(o_ref.dtype)


def ffn_up_conv_gate(a, w_up, layer, conv_w, conv_b, *, n_lat):
    m, k = a.shape
    two_f = w_up.shape[2]
    f = two_f // 2
    tm, tn = _pick_tiles(m, f, lambda tn: 2 * 2 * k * tn * 2,
                         lambda tn: 2 * k * 2 + k * 2 + 2 * tn * 4 + 2 * tn * 2, col_tiles=COL_TILES[:1])
    hb = tm // FFN_HALO
    n_hblk = m // FFN_HALO
    voff = f // tn
    n_sub = tn // MXU_DIM
    b2 = conv_b.reshape(1, two_f)
    u_scratch = [pltpu.VMEM((tm + 2 * FFN_HALO, tn // n_sub), F32) for _ in range(2 * n_sub)]
    return pl.pallas_call(
        functools.partial(_ffn_up_kernel, tm=tm, seg_starts=(0, n_lat), seg_ends=(n_lat - 1, m - 1)),
        out_shape=jax.ShapeDtypeStruct((m, f), BF16),
        grid=(m // tm, f // tn),
        in_specs=[pl.BlockSpec((FFN_HALO, k), lambda i, j: (jnp.maximum(i * hb - 1, 0), 0)),
                  pl.BlockSpec((tm, k), lambda i, j: (i, 0)),
                  pl.BlockSpec((FFN_HALO, k), lambda i, j: (jnp.minimum((i + 1) * hb, n_hblk - 1), 0)),
                  pl.BlockSpec((None, k, tn), lambda i, j: (layer, 0, j)),
                  pl.BlockSpec((None, k, tn), lambda i, j: (layer, 0, j + voff)),
                  pl.BlockSpec((FFN_CONV_WIDTH, tn), lambda i, j: (0, j)),
                  pl.BlockSpec((1, tn), lambda i, j: (0, j)),
                  pl.BlockSpec((FFN_CONV_WIDTH, tn), lambda i, j: (0, j + voff)),
                  pl.BlockSpec((1, tn), lambda i, j: (0, j + voff))],
        out_specs=pl.BlockSpec((tm, tn), lambda i, j: (i, j)),
        scratch_shapes=[pltpu.VMEM((tm + 2 * FFN_HALO, k), BF16)] + u_scratch,
        compiler_params=_cparams(2),
        name="ffn_up_conv_gate",
    )(a, a, a, w_up, w_up, conv_w, b2, conv_w, b2)


def _rope_tables(n_lat, n_ctx):
    t = jnp.arange(n_lat)
    row = (t // GRID_W).astype(F32)
    col = (t % GRID_W).astype(F32)
    per_axis = HEAD_DIM // 2
    inv = ROPE_BASE ** (-jnp.arange(0, per_axis, 2, dtype=F32) / per_axis)
    ang_r, ang_c = row[:, None] * inv, col[:, None] * inv
    cr, sr, cc, sc = jnp.cos(ang_r), jnp.sin(ang_r), jnp.cos(ang_c), jnp.sin(ang_c)
    zero = jnp.zeros_like(sr)
    cos = jnp.concatenate([cr, cr, cc, cc], axis=1)
    sin_a = jnp.concatenate([-sr, zero, -sc, zero], axis=1)
    sin_b = jnp.concatenate([zero, sr, zero, sc], axis=1)
    pad = lambda x, v: jnp.concatenate([x, jnp.full((n_ctx, HEAD_DIM), v, F32)], axis=0)
    return pad(cos, 1.0), pad(sin_a, 0.0), pad(sin_b, 0.0)


def kernel(x, c, ctx, c_ctx, mod_w, mod_b, norm1_w, norm2_w, na_w_in, na_w_out, na_rpb, cv_w, cv_b, cv_ln_w,
           cv_ln_b, diff_w_in, diff_w_out, diff_lambda, diff_subln_w, ffn_w_up, ffn_conv_w, ffn_conv_b,
           ffn_w_down, final_norm_w):
    assert x.shape[0] == 1 and ctx.shape[0] == 1
    n_lat, d = x.shape[1], x.shape[2]
    n_ctx = ctx.shape[1]
    depth = mod_w.shape[0]
    m = n_lat + n_ctx
    na_width = na_rpb.shape[1] * HEAD_DIM
    diff_heads = diff_w_in.shape[2] // (3 * 2 * HEAD_DIM)
    diff_width = diff_heads * 2 * HEAD_DIM

    h = jnp.concatenate([x[0], ctx[0]], axis=0)
    mod = adaln_modulation(jnp.stack([c[0], c_ctx], axis=1), mod_w, mod_b)
    mod = mod.reshape(depth, 2, 6, d)
    cos, sin_a, sin_b = _rope_tables(n_lat, n_ctx)
    ffn_w_up_bf16 = ffn_w_up.astype(BF16)

    for l in range(depth):
        i = l // 2
        shift1, scale1, gate1, shift2, scale2, gate2 = (mod[l, :, j, :] for j in range(6))
        a = norm_modulate(h, norm1_w[l], shift1, scale1, n_lat=n_lat, out_dtype=BF16)
        if l % 2 == 0:
            qkv = matmul(a, na_w_in, i, col0=0, n_cols=3 * na_width, out_dtype=BF16)
            ag = matmul(a, na_w_in, i, col0=3 * na_width, n_cols=na_w_in.shape[2] - 3 * na_width, out_dtype=F32)
            o_na = neighborhood_attention(qkv, na_rpb[i], n_lat=n_lat, n_ctx=n_ctx)
            o_cv = conformer_conv(ag, cv_w[i], cv_b[i], cv_ln_w[i], cv_ln_b[i], n_lat=n_lat)
            h = matmul_residual([o_na, o_cv], na_w_out, i, h, gate1, n_lat=n_lat)
        else:
            lambda_init = 0.8 - 0.6 * math.exp(-0.3 * l)
            qk = matmul_rope(a, diff_w_in, i, cos, sin_a, sin_b, n_cols=2 * diff_width, scaled_cols=diff_width,
                             scale=DIFF_SCORE_SCALE, out_dtype=BF16)
            vt = matmul_transposed_out(a, diff_w_in, i, col0=2 * diff_width, n_cols=diff_width, out_dtype=BF16)
            o_lat = diff_attention(qk, vt, diff_lambda[i], diff_subln_w[i], heads=diff_heads, q_row0=0, n_q=n_lat,
                                   key_row0=0, n_keys=m, lambda_init=lambda_init, tq=min(DIFF_Q_TILE, n_lat))
            o_ctx = diff_attention(qk, vt, diff_lambda[i], diff_subln_w[i], heads=diff_heads, q_row0=n_lat,
                                   n_q=n_ctx, key_row0=n_lat, n_keys=n_ctx, lambda_init=lambda_init, tq=n_ctx)
            o = jnp.concatenate([o_lat, o_ctx], axis=0)
            h = matmul_residual([o], diff_w_out, i, h, gate1, n_lat=n_lat)
        f = norm_modulate(h, norm2_w[l], shift2, scale2, n_lat=n_lat, out_dtype=BF16)
        act = ffn_up_conv_gate(f, ffn_w_up_bf16, l, ffn_conv_w[l], ffn_conv_b[l], n_lat=n_lat)
        h = matmul_residual([act], ffn_w_down, l, h, gate2, n_lat=n_lat)

    zeros2 = jnp.zeros((2, d), F32)
    out = norm_modulate(h, final_norm_w, zeros2, zeros2, n_lat=n_lat, out_dtype=F32, n_rows=n_lat)
    return out[None]
```

```python
import functools
import math

import jax
import jax.numpy as jnp
import numpy as np
from jax import lax
from jax.experimental import pallas as pl
from jax.experimental.pallas import tpu as pltpu

GRID_W = 64
HEAD_DIM = 128
NA_WIN_ROWS = 8
NA_WIN_COLS = 16
CONV_WIDTH = 31
FFN_CONV_WIDTH = 3
ROPE_BASE = 10000.0
EPS = 1e-6

V7X_VMEM_BYTES = 64 * 1024 * 1024
VMEM_LIMIT = V7X_VMEM_BYTES * 7 // 8
LANES = 128
SUBLANES = 8
NEG_INF = -1e30

BF16 = jnp.bfloat16
F32 = jnp.float32


def _cparams(n_grid_axes):
    return pltpu.CompilerParams(dimension_semantics=("arbitrary",) * n_grid_axes,
                                vmem_limit_bytes=VMEM_LIMIT)


def _dot(a, b):
    return jnp.dot(a, b, preferred_element_type=F32)


def _dot_nt(a, b):
    return lax.dot_general(a, b, (((1,), (1,)), ((), ())), preferred_element_type=F32)


def _seg_select(row0, n_rows, n_lat, two_rows):
    rid = row0 + lax.broadcasted_iota(jnp.int32, (n_rows, 1), 0)
    return jnp.where(rid >= n_lat, two_rows[1:2, :], two_rows[0:1, :])


def _mod_kernel(c_ref, w_ref, b_ref, o_ref, s_ref, *, tn):
    k_dim = w_ref.shape[0]

    @pl.when((pl.program_id(0) == 0) & (pl.program_id(1) == 0))
    def _():
        x = c_ref[...]
        s = x * jax.nn.sigmoid(x)
        s_ref[0] = jnp.broadcast_to(s[:, 0:1], (k_dim, LANES))
        s_ref[1] = jnp.broadcast_to(s[:, 1:2], (k_dim, LANES))

    n_groups = tn // LANES

    def body(k, accs):
        r = pl.multiple_of(k * SUBLANES, SUBLANES)
        s0 = s_ref[0, pl.ds(r, SUBLANES), :]
        s1 = s_ref[1, pl.ds(r, SUBLANES), :]
        out = []
        for g in range(n_groups):
            w = w_ref[pl.ds(r, SUBLANES), g * LANES:(g + 1) * LANES]
            out.append(accs[2 * g] + s0 * w)
            out.append(accs[2 * g + 1] + s1 * w)
        return tuple(out)

    zeros = tuple(jnp.zeros((SUBLANES, LANES), F32) for _ in range(2 * n_groups))
    accs = lax.fori_loop(0, k_dim // SUBLANES, body, zeros, unroll=8)
    for g in range(n_groups):
        b = b_ref[:, g * LANES:(g + 1) * LANES]
        o_ref[0:1, g * LANES:(g + 1) * LANES] = jnp.sum(accs[2 * g], axis=0, keepdims=True) + b
        o_ref[1:2, g * LANES:(g + 1) * LANES] = jnp.sum(accs[2 * g + 1], axis=0, keepdims=True) + b


def adaln_modulation(c2, mod_w, mod_b, *, tn=512):
    n_layers, k_dim, n = mod_w.shape
    tn = min(tn, n)
    return pl.pallas_call(
        functools.partial(_mod_kernel, tn=tn),
        out_shape=jax.ShapeDtypeStruct((n_layers, 2, n), F32),
        grid=(n_layers, n // tn),
        in_specs=[
            pl.BlockSpec((k_dim, 2), lambda l, j: (0, 0)),
            pl.BlockSpec((None, k_dim, tn), lambda l, j: (l, 0, j)),
            pl.BlockSpec((None, 1, tn), lambda l, j: (l, 0, j)),
        ],
        out_specs=pl.BlockSpec((None, 2, tn), lambda l, j: (l, 0, j)),
        scratch_shapes=[pltpu.VMEM((2, k_dim, LANES), F32)],
        compiler_params=_cparams(2),
        name="adaln_mod",
    )(c2, mod_w, mod_b.reshape(n_layers, 1, n))


def _norm_mod_kernel(h_ref, w_ref, shift_ref, scale_ref, o_ref):
    x = h_ref[...]
    gain = w_ref[...] * (1.0 + scale_ref[...])
    y = x * lax.rsqrt(jnp.mean(x * x, axis=-1, keepdims=True) + EPS)
    o_ref[...] = (y * gain + shift_ref[...]).astype(o_ref.dtype)


def norm_modulate(h, w, shift2, scale2, *, n_lat, out_dtype, n_rows=None, tm=256):
    m, d = h.shape
    m = m if n_rows is None else n_rows
    tm = min(tm, m)
    assert n_lat % tm == 0, "row tiles must not straddle the latent/context boundary"
    lat_tiles = n_lat // tm
    seg_row = lambda i: (jnp.where(i >= lat_tiles, 1, 0), 0, 0)
    return pl.pallas_call(
        _norm_mod_kernel,
        out_shape=jax.ShapeDtypeStruct((m, d), out_dtype),
        grid=(m // tm,),
        in_specs=[
            pl.BlockSpec((tm, d), lambda i: (i, 0)),
            pl.BlockSpec((1, d), lambda i: (0, 0)),
            pl.BlockSpec((None, 1, d), seg_row),
            pl.BlockSpec((None, 1, d), seg_row),
        ],
        out_specs=pl.BlockSpec((tm, d), lambda i: (i, 0)),
        compiler_params=_cparams(1),
        name="norm_modulate",
    )(h, w.reshape(1, d), shift2.reshape(2, 1, d), scale2.reshape(2, 1, d))


MXU_DIM = 256
ROW_ALIGN = 128
COL_TILES = (512, 256)


def _pick_tiles(m, n, fixed_bytes, row_bytes, col_tiles=COL_TILES):
    best = None
    for tn in col_tiles:
        if n % tn:
            continue
        fits = [t for t in range(ROW_ALIGN, m + 1, ROW_ALIGN)
                if m % t == 0 and fixed_bytes(tn) + t * row_bytes(tn) <= VMEM_LIMIT]
        if fits and (best is None or max(fits) > best[0]):
            best = (max(fits), tn)
    assert best is not None, "no tile fits the VMEM limit"
    return best


def _f32_weight_tile_bytes(k, tn):
    return 2 * k * tn * 4 + k * tn * 2


def _mm_kernel(a_ref, w_ref, o_ref):
    o_ref[...] = _dot(a_ref[...], w_ref[...].astype(BF16)).astype(o_ref.dtype)


def matmul(a, w, layer, *, col0, n_cols, out_dtype):
    m, k = a.shape
    ob = jnp.dtype(out_dtype).itemsize
    tm, tn = _pick_tiles(m, math.gcd(n_cols, col0) if col0 else n_cols,
                         lambda tn: _f32_weight_tile_bytes(k, tn),
                         lambda tn: 2 * k * 2 + 2 * tn * ob + tn * 4)
    blk0 = col0 // tn
    return pl.pallas_call(
        _mm_kernel,
        out_shape=jax.ShapeDtypeStruct((m, n_cols), out_dtype),
        grid=(m // tm, n_cols // tn),
        in_specs=[pl.BlockSpec((tm, k), lambda i, j: (i, 0)),
                  pl.BlockSpec((None, k, tn), lambda i, j: (layer, 0, blk0 + j))],
        out_specs=pl.BlockSpec((tm, tn), lambda i, j: (i, j)),
        compiler_params=_cparams(2),
        name="matmul",
    )(a, w)


def _mm_t_kernel(a_ref, w_ref, o_ref):
    o_ref[...] = _dot(a_ref[...], w_ref[...].astype(BF16)).T.astype(o_ref.dtype)


def matmul_transposed_out(a, w, layer, *, col0, n_cols, out_dtype):
    m, k = a.shape
    ob = jnp.dtype(out_dtype).itemsize
    tm, tn = _pick_tiles(m, math.gcd(n_cols, col0) if col0 else n_cols,
                         lambda tn: _f32_weight_tile_bytes(k, tn),
                         lambda tn: 2 * k * 2 + 2 * tn * ob + 2 * tn * 4)
    blk0 = col0 // tn
    return pl.pallas_call(
        _mm_t_kernel,
        out_shape=jax.ShapeDtypeStruct((n_cols, m), out_dtype),
        grid=(m // tm, n_cols // tn),
        in_specs=[pl.BlockSpec((tm, k), lambda i, j: (i, 0)),
                  pl.BlockSpec((None, k, tn), lambda i, j: (layer, 0, blk0 + j))],
        out_specs=pl.BlockSpec((tn, tm), lambda i, j: (j, i)),
        compiler_params=_cparams(2),
        name="matmul_transposed_out",
    )(a, w)


def _mm_rope_kernel(a_ref, w_ref, cos_ref, sa_ref, sb_ref, o_ref, *, tn, scaled_tiles, scale):
    acc = _dot(a_ref[...], w_ref[...].astype(BF16))
    reps = tn // HEAD_DIM
    cos = jnp.concatenate([cos_ref[...]] * reps, axis=1)
    sa = jnp.concatenate([sa_ref[...]] * reps, axis=1)
    sb = jnp.concatenate([sb_ref[...]] * reps, axis=1)
    up = pltpu.roll(acc, tn - HEAD_DIM // 4, 1)
    dn = pltpu.roll(acc, HEAD_DIM // 4, 1)
    out = acc * cos + up * sa + dn * sb
    out = out * jnp.where(pl.program_id(1) < scaled_tiles, scale, 1.0)
    o_ref[...] = out.astype(o_ref.dtype)


def matmul_rope(a, w, layer, cos, sin_a, sin_b, *, n_cols, scaled_cols, scale, out_dtype):
    m, k = a.shape
    ob = jnp.dtype(out_dtype).itemsize
    tm, tn = _pick_tiles(m, math.gcd(n_cols, scaled_cols),
                         lambda tn: _f32_weight_tile_bytes(k, tn),
                         lambda tn: 2 * k * 2 + 2 * tn * ob + 3 * 2 * HEAD_DIM * 4 + 4 * tn * 4)
    return pl.pallas_call(
        functools.partial(_mm_rope_kernel, tn=tn, scaled_tiles=scaled_cols // tn, scale=scale),
        out_shape=jax.ShapeDtypeStruct((m, n_cols), out_dtype),
        grid=(m // tm, n_cols // tn),
        in_specs=[pl.BlockSpec((tm, k), lambda i, j: (i, 0)),
                  pl.BlockSpec((None, k, tn), lambda i, j: (layer, 0, j)),
                  pl.BlockSpec((tm, HEAD_DIM), lambda i, j: (i, 0)),
                  pl.BlockSpec((tm, HEAD_DIM), lambda i, j: (i, 0)),
                  pl.BlockSpec((tm, HEAD_DIM), lambda i, j: (i, 0))],
        out_specs=pl.BlockSpec((tm, tn), lambda i, j: (i, j)),
        compiler_params=_cparams(2),
        name="matmul_rope",
    )(a, w, cos, sin_a, sin_b)


def _mm_res_kernel(*refs, n_a, n_lat, tm):
    a_refs = refs[:n_a]
    w_ref, res_ref, gate_ref, o_ref = refs[n_a:]
    acc = None
    k0 = 0
    for a_ref in a_refs:
        kk = a_ref.shape[1]
        part = _dot(a_ref[...], w_ref[k0:k0 + kk, :].astype(BF16))
        acc = part if acc is None else acc + part
        k0 += kk
    gate = _seg_select(pl.program_id(0) * tm, tm, n_lat, gate_ref[...])
    o_ref[...] = res_ref[...] + gate * acc


def matmul_residual(a_list, w, layer, res, gate2, *, n_lat):
    m = res.shape[0]
    _, k, n = w.shape
    tm, tn = _pick_tiles(m, n, lambda tn: _f32_weight_tile_bytes(k, tn),
                         lambda tn: 2 * k * 2 + 2 * tn * 4 + 2 * tn * 4 + tn * 4)
    in_specs = [pl.BlockSpec((tm, a.shape[1]), lambda i, j: (i, 0)) for a in a_list]
    in_specs += [pl.BlockSpec((None, k, tn), lambda i, j: (layer, 0, j)),
                 pl.BlockSpec((tm, tn), lambda i, j: (i, j)),
                 pl.BlockSpec((2, tn), lambda i, j: (0, j))]
    return pl.pallas_call(
        functools.partial(_mm_res_kernel, n_a=len(a_list), n_lat=n_lat, tm=tm),
        out_shape=jax.ShapeDtypeStruct((m, n), F32),
        grid=(m // tm, n // tn),
        in_specs=in_specs,
        out_specs=pl.BlockSpec((tm, tn), lambda i, j: (i, j)),
        compiler_params=_cparams(2),
        name="matmul_residual",
    )(*a_list, w, res, gate2)


NA_HEADS_PER_STEP = 4


def _na_window_start(t, rows_per_tile, win_rows, n_rows):
    return jnp.clip(t * rows_per_tile - NA_WIN_ROWS // 2, 0, n_rows - win_rows)


def _na_kernel(pat_ref, q_ref, k_ref, v_ref, kc_ref, vc_ref, bias_ref, o_ref, *,
               rows_per_tile, win_rows, n_rows):
    del pat_ref
    t = pl.program_id(1)
    scale = HEAD_DIM ** -0.5
    start = pl.multiple_of(_na_window_start(t, rows_per_tile, win_rows, n_rows) * GRID_W, GRID_W)
    is_ctx_tile = t == n_rows // rows_per_tile
    for hh in range(NA_HEADS_PER_STEP):
        lanes = slice(hh * HEAD_DIM, (hh + 1) * HEAD_DIM)
        q = q_ref[:, lanes]
        k_loc = k_ref[pl.ds(start, win_rows * GRID_W), lanes]
        v_loc = v_ref[pl.ds(start, win_rows * GRID_W), lanes]
        bias = jnp.where(is_ctx_tile, NEG_INF, bias_ref[hh])
        s_loc = _dot_nt(q, k_loc) * scale + bias
        s_ctx = _dot_nt(q, kc_ref[:, lanes]) * scale
        m = jnp.maximum(jnp.max(s_loc, axis=-1, keepdims=True), jnp.max(s_ctx, axis=-1, keepdims=True))
        p_loc = jnp.exp(s_loc - m)
        p_ctx = jnp.exp(s_ctx - m)
        denom = jnp.sum(p_loc, axis=-1, keepdims=True) + jnp.sum(p_ctx, axis=-1, keepdims=True)
        o = _dot(p_loc.astype(BF16), v_loc) + _dot(p_ctx.astype(BF16), vc_ref[:, lanes])
        o_ref[:, lanes] = (o / denom).astype(o_ref.dtype)


def _na_tile_patterns(n_rows, rows_per_tile):
    win_rows = rows_per_tile + NA_WIN_ROWS - 1
    keys, pat_of_tile, rep_tile = {}, [], []
    for t in range(n_rows // rows_per_tile):
        r0 = t * rows_per_tile
        us = min(max(r0 - NA_WIN_ROWS // 2, 0), n_rows - win_rows)
        rs = tuple(min(max(r0 + i - NA_WIN_ROWS // 2, 0), n_rows - NA_WIN_ROWS) - us
                   for i in range(rows_per_tile))
        key = (r0 - us, rs)
        if key not in keys:
            keys[key] = len(keys)
            rep_tile.append(t)
        pat_of_tile.append(keys[key])
    return np.asarray(pat_of_tile, np.int32), rep_tile, win_rows


def _na_bias_table(rpb, n_rows, rows_per_tile):
    pat_of_tile, rep_tile, win_rows = _na_tile_patterns(n_rows, rows_per_tile)
    heads = rpb.shape[0]
    w = np.arange(GRID_W)
    col_start = np.clip(w - NA_WIN_COLS // 2, 0, GRID_W - NA_WIN_COLS)
    col_valid = (w[None, :] >= col_start[:, None]) & (w[None, :] < col_start[:, None] + NA_WIN_COLS)
    col_off = w[None, :] - w[:, None] + NA_WIN_COLS - 1
    col_sel = (col_off[:, :, None] == np.arange(2 * NA_WIN_COLS - 1)) & col_valid[:, :, None]
    row_sel, valid = [], []
    for t in rep_tile:
        r0 = t * rows_per_tile
        us = min(max(r0 - NA_WIN_ROWS // 2, 0), n_rows - win_rows)
        r = r0 + np.arange(rows_per_tile)
        rs = np.clip(r - NA_WIN_ROWS // 2, 0, n_rows - NA_WIN_ROWS)
        key_row = us + np.arange(win_rows)
        row_valid = (key_row[None, :] >= rs[:, None]) & (key_row[None, :] < rs[:, None] + NA_WIN_ROWS)
        row_off = key_row[None, :] - r[:, None] + NA_WIN_ROWS - 1
        row_sel.append((row_off[:, :, None] == np.arange(2 * NA_WIN_ROWS - 1)) & row_valid[:, :, None])
        valid.append(row_valid[:, None, :, None] & col_valid[None, :, None, :])
    row_sel = jnp.asarray(np.stack(row_sel), F32)
    valid = jnp.asarray(np.stack(valid))
    toeplitz = jnp.einsum('hrc,wvc->hrwv', rpb.astype(F32), jnp.asarray(col_sel, F32),
                          precision=lax.Precision.HIGHEST)
    bias = jnp.einsum('pijr,hrwv->phiwjv', row_sel, toeplitz, precision=lax.Precision.HIGHEST)
    bias = jnp.where(valid[:, None], bias, NEG_INF)
    bias = bias.reshape(len(rep_tile), heads, rows_per_tile * GRID_W, win_rows * GRID_W)
    return bias, pat_of_tile, win_rows


def neighborhood_attention(qkv, rpb, *, n_lat, n_ctx, rows_per_tile=4):
    heads = rpb.shape[0]
    n_rows = n_lat // GRID_W
    bias, pat_of_tile, win_rows = _na_bias_table(rpb, n_rows, rows_per_tile)
    tq = rows_per_tile * GRID_W
    assert n_ctx == tq, "the context rows must fill exactly one query tile"
    pat_of_tile = np.concatenate([pat_of_tile, np.zeros((1,), np.int32)])
    ctx_blk = n_lat // n_ctx
    hps = NA_HEADS_PER_STEP
    groups = heads // hps
    wide = hps * HEAD_DIM
    grid_spec = pltpu.PrefetchScalarGridSpec(
        num_scalar_prefetch=1,
        grid=(groups, n_rows // rows_per_tile + 1),
        in_specs=[
            pl.BlockSpec((tq, wide), lambda g, t, pat: (t, g)),
            pl.BlockSpec((n_lat, wide), lambda g, t, pat: (0, groups + g)),
            pl.BlockSpec((n_lat, wide), lambda g, t, pat: (0, 2 * groups + g)),
            pl.BlockSpec((n_ctx, wide), lambda g, t, pat: (ctx_blk, groups + g)),
            pl.BlockSpec((n_ctx, wide), lambda g, t, pat: (ctx_blk, 2 * groups + g)),
            pl.BlockSpec((None, hps, tq, win_rows * GRID_W), lambda g, t, pat: (pat[t], g, 0, 0)),
        ],
        out_specs=pl.BlockSpec((tq, wide), lambda g, t, pat: (t, g)),
    )
    return pl.pallas_call(
        functools.partial(_na_kernel, rows_per_tile=rows_per_tile, win_rows=win_rows, n_rows=n_rows),
        out_shape=jax.ShapeDtypeStruct((n_lat + n_ctx, heads * HEAD_DIM), BF16),
        grid_spec=grid_spec,
        compiler_params=_cparams(2),
        name="neighborhood_attention",
    )(jnp.asarray(pat_of_tile), qkv, qkv, qkv, qkv, qkv, bias)


CONV_HALO = 16


def _conformer_kernel(ap_ref, gp_ref, a_ref, g_ref, an_ref, gn_ref, cw_ref, cb_ref, lw_ref, lb_ref, o_ref,
                      buf_ref, y_ref, sh_ref, *, tm, first_tiles, last_tiles):
    i = pl.program_id(0)
    ch = a_ref.shape[1]
    has_prev = jnp.logical_not(functools.reduce(jnp.logical_or, [i == t for t in first_tiles]))
    has_next = jnp.logical_not(functools.reduce(jnp.logical_or, [i == t for t in last_tiles]))

    def glu(a, g):
        return a * jax.nn.sigmoid(g)

    buf_ref[0:CONV_HALO, :] = jnp.where(has_prev, glu(ap_ref[...], gp_ref[...]), 0.0)
    buf_ref[CONV_HALO:CONV_HALO + tm, :] = glu(a_ref[...], g_ref[...])
    buf_ref[CONV_HALO + tm:, :] = jnp.where(has_next, glu(an_ref[...], gn_ref[...]), 0.0)

    base = CONV_HALO - CONV_WIDTH // 2

    n_sh = sh_ref.shape[1]

    def chunk(c, carry):
        col = pl.multiple_of(c * LANES, LANES)
        for phase in range(SUBLANES):
            sh_ref[phase] = buf_ref[phase:phase + n_sh, pl.ds(col, LANES)]
        acc = jnp.zeros((tm, LANES), F32) + cb_ref[:, pl.ds(col, LANES)]
        for tap in range(CONV_WIDTH):
            row, phase = divmod(base + tap, SUBLANES)
            row *= SUBLANES
            acc = acc + sh_ref[phase, row:row + tm, :] * cw_ref[tap:tap + 1, pl.ds(col, LANES)]
        y_ref[:, pl.ds(col, LANES)] = acc
        return carry

    lax.fori_loop(0, ch // LANES, chunk, 0)

    y = y_ref[...]
    mu = jnp.mean(y, axis=-1, keepdims=True)
    yc = y - mu
    z = yc * lax.rsqrt(jnp.mean(yc * yc, axis=-1, keepdims=True) + EPS)
    z = z * lw_ref[...] + lb_ref[...]
    o_ref[...] = (z * jax.nn.sigmoid(z)).astype(o_ref.dtype)


def conformer_conv(ag, cv_w, cv_b, ln_w, ln_b, *, n_lat, tm=256):
    m, two_c = ag.shape
    ch = two_c // 2
    n_tiles = m // tm
    hb = tm // CONV_HALO
    n_hblk = m // CONV_HALO
    first_tiles = (0, n_lat // tm)
    last_tiles = (n_lat // tm - 1, n_tiles - 1)
    prev_map = lambda c: (lambda i: (jnp.maximum(i * hb - 1, 0), c))
    next_map = lambda c: (lambda i: (jnp.minimum((i + 1) * hb, n_hblk - 1), c))
    cur_map = lambda c: (lambda i: (i, c))
    vec = lambda: pl.BlockSpec((1, ch), lambda i: (0, 0))
    return pl.pallas_call(
        functools.partial(_conformer_kernel, tm=tm, first_tiles=first_tiles, last_tiles=last_tiles),
        out_shape=jax.ShapeDtypeStruct((m, ch), BF16),
        grid=(n_tiles,),
        in_specs=[pl.BlockSpec((CONV_HALO, ch), prev_map(0)), pl.BlockSpec((CONV_HALO, ch), prev_map(1)),
                  pl.BlockSpec((tm, ch), cur_map(0)), pl.BlockSpec((tm, ch), cur_map(1)),
                  pl.BlockSpec((CONV_HALO, ch), next_map(0)), pl.BlockSpec((CONV_HALO, ch), next_map(1)),
                  pl.BlockSpec((CONV_WIDTH, ch), lambda i: (0, 0)), vec(), vec(), vec()],
        out_specs=pl.BlockSpec((tm, ch), lambda i: (i, 0)),
        scratch_shapes=[pltpu.VMEM((tm + 2 * CONV_HALO, ch), F32), pltpu.VMEM((tm, ch), F32),
                        pltpu.VMEM((SUBLANES, tm + 2 * CONV_HALO - SUBLANES, LANES), F32)],
        compiler_params=_cparams(1),
        name="conformer_conv",
    )(ag, ag, ag, ag, ag, ag, cv_w, cv_b.reshape(1, ch), ln_w.reshape(1, ch), ln_b.reshape(1, ch))


DIFF_Q_TILE = 256
DIFF_KV_CHUNK = 1536
DIFF_KV_LAST_CHUNK = 768
LOG2E = 1.4426950408889634
DIFF_SCORE_SCALE = HEAD_DIM ** -0.5 * LOG2E


def _diff_attn_kernel(lam_ref, q_ref, k_ref, vt_ref, sw_ref, o_ref, acc1_ref, acc2_ref, *,
                      chunks, lambda_init):
    n_chunks = len(chunks)
    q = (q_ref[:, :HEAD_DIM], q_ref[:, HEAD_DIM:])
    acc_refs = (acc1_ref, acc2_ref)
    tq = q_ref.shape[0]
    m = [jnp.full((1, tq), NEG_INF, F32)] * 2
    l = [jnp.zeros((1, tq), F32)] * 2

    def scores(ci):
        row, rows = chunks[ci]
        kc = k_ref[row:row + rows, :]
        return [_dot_nt(kc[:, b * HEAD_DIM:(b + 1) * HEAD_DIM], q[b]) for b in range(2)]

    def softmax_step(s, b):
        m_new = jnp.maximum(m[b], jnp.max(s, axis=0, keepdims=True))
        alpha = jnp.exp2(m[b] - m_new)
        p = jnp.exp2(s - m_new)
        l[b] = alpha * l[b] + jnp.sum(p, axis=0, keepdims=True)
        m[b] = m_new
        return p.astype(BF16), alpha

    def values(ci, p, alpha, b):
        row, rows = chunks[ci]
        vtc = vt_ref[:, row:row + rows]
        pv = _dot(vtc, p)
        acc_refs[b][...] = pv if ci == 0 else alpha * acc_refs[b][...] + pv

    s_of, p_of = {}, {}
    for t in range(n_chunks + 2):
        if t < n_chunks:
            s_of[t] = scores(t)
        if 0 <= t - 1 < n_chunks:
            p_of[t - 1] = [softmax_step(s, b) for b, s in enumerate(s_of.pop(t - 1))]
        if 0 <= t - 2 < n_chunks:
            for b, (p, alpha) in enumerate(p_of.pop(t - 2)):
                values(t - 2, p, alpha, b)

    lp = lam_ref[...]
    lam = (jnp.exp(jnp.sum(lp[0:1] * lp[1:2], axis=-1, keepdims=True))
           - jnp.exp(jnp.sum(lp[2:3] * lp[3:4], axis=-1, keepdims=True)) + lambda_init)
    o = acc1_ref[...] * (1.0 / l[0]) - acc2_ref[...] * (lam / l[1])
    o = o * lax.rsqrt(jnp.mean(o * o, axis=0, keepdims=True) + EPS)
    o = (o * sw_ref[...]) * (1.0 - lambda_init)
    o_ref[...] = o.T.astype(o_ref.dtype)


def _kv_chunks(key_row0, n_keys):
    assert n_keys % MXU_DIM == 0
    sizes = []
    left = n_keys
    while left > DIFF_KV_CHUNK + DIFF_KV_LAST_CHUNK:
        sizes.append(DIFF_KV_CHUNK)
        left -= DIFF_KV_CHUNK
    if left > DIFF_KV_CHUNK:
        sizes.append(left - DIFF_KV_LAST_CHUNK)
        left = DIFF_KV_LAST_CHUNK
    sizes.append(left)
    starts = np.cumsum([0] + sizes[:-1])
    return tuple((key_row0 + int(s), int(n)) for s, n in zip(starts, sizes))


def diff_attention(qk, vt, lam_p, subln_w, *, heads, q_row0, n_q, key_row0, n_keys, lambda_init, tq):
    m = qk.shape[0]
    hd2 = 2 * HEAD_DIM
    q_blk0 = q_row0 // tq
    scratch = [pltpu.VMEM((hd2, tq), F32), pltpu.VMEM((hd2, tq), F32)]
    return pl.pallas_call(
        functools.partial(_diff_attn_kernel, chunks=_kv_chunks(key_row0, n_keys), lambda_init=lambda_init),
        out_shape=jax.ShapeDtypeStruct((n_q, heads * hd2), BF16),
        grid=(heads, n_q // tq),
        in_specs=[pl.BlockSpec((4, HEAD_DIM), lambda h, i: (0, 0)),
                  pl.BlockSpec((tq, hd2), lambda h, i: (q_blk0 + i, h)),
                  pl.BlockSpec((m, hd2), lambda h, i: (0, heads + h)),
                  pl.BlockSpec((hd2, m), lambda h, i: (h, 0)),
                  pl.BlockSpec((hd2, 1), lambda h, i: (0, 0))],
        out_specs=pl.BlockSpec((tq, hd2), lambda h, i: (i, h)),
        scratch_shapes=scratch,
        compiler_params=_cparams(2),
        name="diff_attention",
    )(lam_p.astype(F32), qk, qk, vt, subln_w.reshape(hd2, 1))


FFN_HALO = 16


def _ffn_up_kernel(ap_ref, a_ref, an_ref, wg_ref, wv_ref, cwg_ref, cbg_ref, cwv_ref, cbv_ref, o_ref,
                   aext_ref, *u_refs, tm, seg_starts, seg_ends):
    @pl.when(pl.program_id(1) == 0)
    def _():
        aext_ref[0:FFN_HALO, :] = ap_ref[...]
        aext_ref[FFN_HALO:FFN_HALO + tm, :] = a_ref[...]
        aext_ref[FFN_HALO + tm:, :] = an_ref[...]

    rid = pl.program_id(0) * tm + lax.broadcasted_iota(jnp.int32, (tm, 1), 0)
    has_prev = functools.reduce(jnp.logical_and, [rid != r for r in seg_starts])
    has_next = functools.reduce(jnp.logical_and, [rid != r for r in seg_ends])

    def conv(u_ref, w, b):
        prev = jnp.where(has_prev, u_ref[FFN_HALO - 1:FFN_HALO - 1 + tm, :], 0.0)
        nxt = jnp.where(has_next, u_ref[FFN_HALO + 1:FFN_HALO + 1 + tm, :], 0.0)
        return b + prev * w[0:1, :] + u_ref[FFN_HALO:FFN_HALO + tm, :] * w[1:2, :] + nxt * w[2:3, :]

    a_ext = aext_ref[...]
    n_sub = len(u_refs) // 2
    sub = o_ref.shape[1] // n_sub
    for s in range(n_sub):
        cols = slice(s * sub, (s + 1) * sub)
        ug_ref, uv_ref = u_refs[2 * s], u_refs[2 * s + 1]
        ug_ref[...] = _dot(a_ext, wg_ref[:, cols])
        uv_ref[...] = _dot(a_ext, wv_ref[:, cols])
        gate = conv(ug_ref, cwg_ref[:, cols], cbg_ref[:, cols])
        val = conv(uv_ref, cwv_ref[:, cols], cbv_ref[:, cols])
        o_ref[:, cols] = (gate * jax.nn.sigmoid(gate) * val).astype(o_ref.dtype)


def ffn_up_conv_gate(a, w_up, layer, conv_w, conv_b, *, n_lat):
    m, k = a.shape
    two_f = w_up.shape[2]
    f = two_f // 2
    tm, tn = _pick_tiles(m, f, lambda tn: 2 * 2 * k * tn * 2,
                         lambda tn: 2 * k * 2 + k * 2 + 2 * tn * 4 + 2 * tn * 2, col_tiles=COL_TILES[:1])
    hb = tm // FFN_HALO
    n_hblk = m // FFN_HALO
    voff = f // tn
    n_sub = tn // MXU_DIM
    b2 = conv_b.reshape(1, two_f)
    u_scratch = [pltpu.VMEM((tm + 2 * FFN_HALO, tn // n_sub), F32) for _ in range(2 * n_sub)]
    return pl.pallas_call(
        functools.partial(_ffn_up_kernel, tm=tm, seg_starts=(0, n_lat), seg_ends=(n_lat - 1, m - 1)),
        out_shape=jax.ShapeDtypeStruct((m, f), BF16),
        grid=(m // tm, f // tn),
        in_specs=[pl.BlockSpec((FFN_HALO, k), lambda i, j: (jnp.maximum(i * hb - 1, 0), 0)),
                  pl.BlockSpec((tm, k), lambda i, j: (i, 0)),
                  pl.BlockSpec((FFN_HALO, k), lambda i, j: (jnp.minimum((i + 1) * hb, n_hblk - 1), 0)),
                  pl.BlockSpec((None, k, tn), lambda i, j: (layer, 0, j)),
                  pl.BlockSpec((None, k, tn), lambda i, j: (layer, 0, j + voff)),
                  pl.BlockSpec((FFN_CONV_WIDTH, tn), lambda i, j: (0, j)),
                  pl.BlockSpec((1, tn), lambda i, j: (0, j)),
                  pl.BlockSpec((FFN_CONV_WIDTH, tn), lambda i, j: (0, j + voff)),
                  pl.BlockSpec((1, tn), lambda i, j: (0, j + voff))],
        out_specs=pl.BlockSpec((tm, tn), lambda i, j: (i, j)),
        scratch_shapes=[pltpu.VMEM((tm + 2 * FFN_HALO, k), BF16)] + u_scratch,
        compiler_params=_cparams(2),
        name="ffn_up_conv_gate",
    )(a, a, a, w_up, w_up, conv_w, b2, conv_w, b2)


def _rope_tables(n_lat, n_ctx):
    t = jnp.arange(n_lat)
    row = (t // GRID_W).astype(F32)
    col = (t % GRID_W).astype(F32)
    per_axis = HEAD_DIM // 2
    inv = ROPE_BASE ** (-jnp.arange(0, per_axis, 2, dtype=F32) / per_axis)
    ang_r, ang_c = row[:, None] * inv, col[:, None] * inv
    cr, sr, cc, sc = jnp.cos(ang_r), jnp.sin(ang_r), jnp.cos(ang_c), jnp.sin(ang_c)
    zero = jnp.zeros_like(sr)
    cos = jnp.concatenate([cr, cr, cc, cc], axis=1)
    sin_a = jnp.concatenate([-sr, zero, -sc, zero], axis=1)
    sin_b = jnp.concatenate([zero, sr, zero, sc], axis=1)
    pad = lambda x, v: jnp.concatenate([x, jnp.full((n_ctx, HEAD_DIM), v, F32)], axis=0)
    return pad(cos, 1.0), pad(sin_a, 0.0), pad(sin_b, 0.0)


def kernel(x, c, ctx, c_ctx, mod_w, mod_b, norm1_w, norm2_w, na_w_in, na_w_out, na_rpb, cv_w, cv_b, cv_ln_w,
           cv_ln_b, diff_w_in, diff_w_out, diff_lambda, diff_subln_w, ffn_w_up, ffn_conv_w, ffn_conv_b,
           ffn_w_down, final_norm_w):
    assert x.shape[0] == 1 and ctx.shape[0] == 1
    n_lat, d = x.shape[1], x.shape[2]
    n_ctx = ctx.shape[1]
    depth = mod_w.shape[0]
    m = n_lat + n_ctx
    na_width = na_rpb.shape[1] * HEAD_DIM
    diff_heads = diff_w_in.shape[2] // (3 * 2 * HEAD_DIM)
    diff_width = diff_heads * 2 * HEAD_DIM

    h = jnp.concatenate([x[0], ctx[0]], axis=0)
    mod = adaln_modulation(jnp.stack([c[0], c_ctx], axis=1), mod_w, mod_b)
    mod = mod.reshape(depth, 2, 6, d)
    cos, sin_a, sin_b = _rope_tables(n_lat, n_ctx)
    ffn_w_up_bf16 = ffn_w_up.astype(BF16)

    for l in range(depth):
        i = l // 2
        shift1, scale1, gate1, shift2, scale2, gate2 = (mod[l, :, j, :] for j in range(6))
        a = norm_modulate(h, norm1_w[l], shift1, scale1, n_lat=n_lat, out_dtype=BF16)
        if l % 2 == 0:
            qkv = matmul(a, na_w_in, i, col0=0, n_cols=3 * na_width, out_dtype=BF16)
            ag = matmul(a, na_w_in, i, col0=3 * na_width, n_cols=na_w_in.shape[2] - 3 * na_width, out_dtype=F32)
            o_na = neighborhood_attention(qkv, na_rpb[i], n_lat=n_lat, n_ctx=n_ctx)
            o_cv = conformer_conv(ag, cv_w[i], cv_b[i], cv_ln_w[i], cv_ln_b[i], n_lat=n_lat)
            h = matmul_residual([o_na, o_cv], na_w_out, i, h, gate1, n_lat=n_lat)
        else:
            lambda_init = 0.8 - 0.6 * math.exp(-0.3 * l)
            qk = matmul_rope(a, diff_w_in, i, cos, sin_a, sin_b, n_cols=2 * diff_width, scaled_cols=diff_width,
                             scale=DIFF_SCORE_SCALE, out_dtype=BF16)
            vt = matmul_transposed_out(a, diff_w_in, i, col0=2 * diff_width, n_cols=diff_width, out_dtype=BF16)
            o_lat = diff_attention(qk, vt, diff_lambda[i], diff_subln_w[i], heads=diff_heads, q_row0=0, n_q=n_lat,
                                   key_row0=0, n_keys=m, lambda_init=lambda_init, tq=min(DIFF_Q_TILE, n_lat))
            o_ctx = diff_attention(qk, vt, diff_lambda[i], diff_subln_w[i], heads=diff_heads, q_row0=n_lat,
                                   n_q=n_ctx, key_row0=n_lat, n_keys=n_ctx, lambda_init=lambda_init, tq=n_ctx)
            o = jnp.concatenate([o_lat, o_ctx], axis=0)
            h = matmul_residual([o], diff_w_out, i, h, gate1, n_lat=n_lat)
        f = norm_modulate(h, norm2_w[l], shift2, scale2, n_lat=n_lat, out_dtype=BF16)
        act = ffn_up_conv_gate(f, ffn_w_up_bf16, l, ffn_conv_w[l], ffn_conv_b[l], n_lat=n_lat)
        h = matmul_residual([act], ffn_w_down, l, h, gate2, n_lat=n_lat)

    zeros2 = jnp.zeros((2, d), F32)
    out = norm_modulate(h, final_norm_w, zeros2, zeros2, n_lat=n_lat, out_dtype=F32, n_rows=n_lat)
    return out[None]
```

```python
import functools
import math

import jax
import jax.numpy as jnp
import numpy as np
from jax import lax
from jax.experimental import pallas as pl
from jax.experimental.pallas import tpu as pltpu

GRID_W = 64
HEAD_DIM = 128
NA_WIN_ROWS = 8
NA_WIN_COLS = 16
CONV_WIDTH = 31
FFN_CONV_WIDTH = 3
ROPE_BASE = 10000.0
EPS = 1e-6

V7X_VMEM_BYTES = 64 * 1024 * 1024
VMEM_LIMIT = V7X_VMEM_BYTES * 7 // 8
LANES = 128
SUBLANES = 8
NEG_INF = -1e30

BF16 = jnp.bfloat16
F32 = jnp.float32


def _cparams(n_grid_axes):
    return pltpu.CompilerParams(dimension_semantics=("arbitrary",) * n_grid_axes,
                                vmem_limit_bytes=VMEM_LIMIT)


def _dot(a, b):
    return jnp.dot(a, b, preferred_element_type=F32)


def _dot_nt(a, b):
    return lax.dot_general(a, b, (((1,), (1,)), ((), ())), preferred_element_type=F32)


def _seg_select(row0, n_rows, n_lat, two_rows):
    rid = row0 + lax.broadcasted_iota(jnp.int32, (n_rows, 1), 0)
    return jnp.where(rid >= n_lat, two_rows[1:2, :], two_rows[0:1, :])


def _mod_kernel(c_ref, w_ref, b_ref, o_ref, s_ref, *, tn):
    k_dim = w_ref.shape[0]

    @pl.when((pl.program_id(0) == 0) & (pl.program_id(1) == 0))
    def _():
        x = c_ref[...]
        s = x * jax.nn.sigmoid(x)
        s_ref[0] = jnp.broadcast_to(s[:, 0:1], (k_dim, LANES))
        s_ref[1] = jnp.broadcast_to(s[:, 1:2], (k_dim, LANES))

    n_groups = tn // LANES

    def body(k, accs):
        r = pl.multiple_of(k * SUBLANES, SUBLANES)
        s0 = s_ref[0, pl.ds(r, SUBLANES), :]
        s1 = s_ref[1, pl.ds(r, SUBLANES), :]
        out = []
        for g in range(n_groups):
            w = w_ref[pl.ds(r, SUBLANES), g * LANES:(g + 1) * LANES]
            out.append(accs[2 * g] + s0 * w)
            out.append(accs[2 * g + 1] + s1 * w)
        return tuple(out)

    zeros = tuple(jnp.zeros((SUBLANES, LANES), F32) for _ in range(2 * n_groups))
    accs = lax.fori_loop(0, k_dim // SUBLANES, body, zeros, unroll=8)
    for g in range(n_groups):
        b = b_ref[:, g * LANES:(g + 1) * LANES]
        o_ref[0:1, g * LANES:(g + 1) * LANES] = jnp.sum(accs[2 * g], axis=0, keepdims=True) + b
        o_ref[1:2, g * LANES:(g + 1) * LANES] = jnp.sum(accs[2 * g + 1], axis=0, keepdims=True) + b


def adaln_modulation(c2, mod_w, mod_b, *, tn=512):
    n_layers, k_dim, n = mod_w.shape
    tn = min(tn, n)
    return pl.pallas_call(
        functools.partial(_mod_kernel, tn=tn),
        out_shape=jax.ShapeDtypeStruct((n_layers, 2, n), F32),
        grid=(n_layers, n // tn),
        in_specs=[
            pl.BlockSpec((k_dim, 2), lambda l, j: (0, 0)),
            pl.BlockSpec((None, k_dim, tn), lambda l, j: (l, 0, j)),
            pl.BlockSpec((None, 1, tn), lambda l, j: (l, 0, j)),
        ],
        out_specs=pl.BlockSpec((None, 2, tn), lambda l, j: (l, 0, j)),
        scratch_shapes=[pltpu.VMEM((2, k_dim, LANES), F32)],
        compiler_params=_cparams(2),
        name="adaln_mod",
    )(c2, mod_w, mod_b.reshape(n_layers, 1, n))


def _norm_mod_kernel(h_ref, w_ref, shift_ref, scale_ref, o_ref):
    x = h_ref[...]
    gain = w_ref[...] * (1.0 + scale_ref[...])
    y = x * lax.rsqrt(jnp.mean(x * x, axis=-1, keepdims=True) + EPS)
    o_ref[...] = (y * gain + shift_ref[...]).astype(o_ref.dtype)


def norm_modulate(h, w, shift2, scale2, *, n_lat, out_dtype, n_rows=None, tm=256):
    m, d = h.shape
    m = m if n_rows is None else n_rows
    tm = min(tm, m)
    assert n_lat % tm == 0, "row tiles must not straddle the latent/context boundary"
    lat_tiles = n_lat // tm
    seg_row = lambda i: (jnp.where(i >= lat_tiles, 1, 0), 0, 0)
    return pl.pallas_call(
        _norm_mod_kernel,
        out_shape=jax.ShapeDtypeStruct((m, d), out_dtype),
        grid=(m // tm,),
        in_specs=[
            pl.BlockSpec((tm, d), lambda i: (i, 0)),
            pl.BlockSpec((1, d), lambda i: (0, 0)),
            pl.BlockSpec((None, 1, d), seg_row),
            pl.BlockSpec((None, 1, d), seg_row),
        ],
        out_specs=pl.BlockSpec((tm, d), lambda i: (i, 0)),
        compiler_params=_cparams(1),
        name="norm_modulate",
    )(h, w.reshape(1, d), shift2.reshape(2, 1, d), scale2.reshape(2, 1, d))


MXU_DIM = 256
ROW_ALIGN = 128
COL_TILES = (512, 256)


def _pick_tiles(m, n, fixed_bytes, row_bytes, col_tiles=COL_TILES):
    best = None
    for tn in col_tiles:
        if n % tn:
            continue
        fits = [t for t in range(ROW_ALIGN, m + 1, ROW_ALIGN)
                if m % t == 0 and fixed_bytes(tn) + t * row_bytes(tn) <= VMEM_LIMIT]
        if fits and (best is None or max(fits) > best[0]):
            best = (max(fits), tn)
    assert best is not None, "no tile fits the VMEM limit"
    return best


def _f32_weight_tile_bytes(k, tn):
    return 2 * k * tn * 4 + k * tn * 2


def _mm_kernel(a_ref, w_ref, o_ref):
    o_ref[...] = _dot(a_ref[...], w_ref[...].astype(BF16)).astype(o_ref.dtype)


def matmul(a, w, layer, *, col0, n_cols, out_dtype):
    m, k = a.shape
    ob = jnp.dtype(out_dtype).itemsize
    tm, tn = _pick_tiles(m, math.gcd(n_cols, col0) if col0 else n_cols,
                         lambda tn: _f32_weight_tile_bytes(k, tn),
                         lambda tn: 2 * k * 2 + 2 * tn * ob + tn * 4)
    blk0 = col0 // tn
    return pl.pallas_call(
        _mm_kernel,
        out_shape=jax.ShapeDtypeStruct((m, n_cols), out_dtype),
        grid=(m // tm, n_cols // tn),
        in_specs=[pl.BlockSpec((tm, k), lambda i, j: (i, 0)),
                  pl.BlockSpec((None, k, tn), lambda i, j: (layer, 0, blk0 + j))],
        out_specs=pl.BlockSpec((tm, tn), lambda i, j: (i, j)),
        compiler_params=_cparams(2),
        name="matmul",
    )(a, w)


def _mm_t_kernel(a_ref, w_ref, o_ref):
    o_ref[...] = _dot(a_ref[...], w_ref[...].astype(BF16)).T.astype(o_ref.dtype)


def matmul_transposed_out(a, w, layer, *, col0, n_cols, out_dtype):
    m, k = a.shape
    ob = jnp.dtype(out_dtype).itemsize
    tm, tn = _pick_tiles(m, math.gcd(n_cols, col0) if col0 else n_cols,
                         lambda tn: _f32_weight_tile_bytes(k, tn),
                         lambda tn: 2 * k * 2 + 2 * tn * ob + 2 * tn * 4)
    blk0 = col0 // tn
    return pl.pallas_call(
        _mm_t_kernel,
        out_shape=jax.ShapeDtypeStruct((n_cols, m), out_dtype),
        grid=(m // tm, n_cols // tn),
        in_specs=[pl.BlockSpec((tm, k), lambda i, j: (i, 0)),
                  pl.BlockSpec((None, k, tn), lambda i, j: (layer, 0, blk0 + j))],
        out_specs=pl.BlockSpec((tn, tm), lambda i, j: (j, i)),
        compiler_params=_cparams(2),
        name="matmul_transposed_out",
    )(a, w)


def _mm_rope_kernel(a_ref, w_ref, cos_ref, sa_ref, sb_ref, o_ref, *, tn, scaled_tiles, scale):
    sub = MXU_DIM
    reps = sub // HEAD_DIM
    cos = jnp.concatenate([cos_ref[...]] * reps, axis=1)
    sa = jnp.concatenate([sa_ref[...]] * reps, axis=1)
    sb = jnp.concatenate([sb_ref[...]] * reps, axis=1)
    col_scale = jnp.where(pl.program_id(1) < scaled_tiles, scale, 1.0)
    a = a_ref[...]
    for s in range(tn // sub):
        cols = slice(s * sub, (s + 1) * sub)
        acc = _dot(a, w_ref[:, cols].astype(BF16))
        up = pltpu.roll(acc, sub - HEAD_DIM // 4, 1)
        dn = pltpu.roll(acc, HEAD_DIM // 4, 1)
        o_ref[:, cols] = ((acc * cos + up * sa + dn * sb) * col_scale).astype(o_ref.dtype)


def matmul_rope(a, w, layer, cos, sin_a, sin_b, *, n_cols, scaled_cols, scale, out_dtype):
    m, k = a.shape
    ob = jnp.dtype(out_dtype).itemsize
    tm, tn = _pick_tiles(m, math.gcd(n_cols, scaled_cols),
                         lambda tn: _f32_weight_tile_bytes(k, tn),
                         lambda tn: 2 * k * 2 + 2 * tn * ob + 3 * 2 * HEAD_DIM * 4 + 4 * MXU_DIM * 4,
                         col_tiles=COL_TILES[:1])
    return pl.pallas_call(
        functools.partial(_mm_rope_kernel, tn=tn, scaled_tiles=scaled_cols // tn, scale=scale),
        out_shape=jax.ShapeDtypeStruct((m, n_cols), out_dtype),
        grid=(m // tm, n_cols // tn),
        in_specs=[pl.BlockSpec((tm, k), lambda i, j: (i, 0)),
                  pl.BlockSpec((None, k, tn), lambda i, j: (layer, 0, j)),
                  pl.BlockSpec((tm, HEAD_DIM), lambda i, j: (i, 0)),
                  pl.BlockSpec((tm, HEAD_DIM), lambda i, j: (i, 0)),
                  pl.BlockSpec((tm, HEAD_DIM), lambda i, j: (i, 0))],
        out_specs=pl.BlockSpec((tm, tn), lambda i, j: (i, j)),
        compiler_params=_cparams(2),
        name="matmul_rope",
    )(a, w, cos, sin_a, sin_b)


def _mm_res_kernel(*refs, n_a, n_lat, tm):
    a_refs = refs[:n_a]
    w_ref, res_ref, gate_ref, o_ref = refs[n_a:]
    acc = None
    k0 = 0
    for a_ref in a_refs:
        kk = a_ref.shape[1]
        part = _dot(a_ref[...], w_ref[k0:k0 + kk, :].astype(BF16))
        acc = part if acc is None else acc + part
        k0 += kk
    gate = _seg_select(pl.program_id(0) * tm, tm, n_lat, gate_ref[...])
    o_ref[...] = res_ref[...] + gate * acc


def matmul_residual(a_list, w, layer, res, gate2, *, n_lat):
    m = res.shape[0]
    _, k, n = w.shape
    tm, tn = _pick_tiles(m, n, lambda tn: _f32_weight_tile_bytes(k, tn),
                         lambda tn: 2 * k * 2 + 2 * tn * 4 + 2 * tn * 4 + tn * 4)
    in_specs = [pl.BlockSpec((tm, a.shape[1]), lambda i, j: (i, 0)) for a in a_list]
    in_specs += [pl.BlockSpec((None, k, tn), lambda i, j: (layer, 0, j)),
                 pl.BlockSpec((tm, tn), lambda i, j: (i, j)),
                 pl.BlockSpec((2, tn), lambda i, j: (0, j))]
    return pl.pallas_call(
        functools.partial(_mm_res_kernel, n_a=len(a_list), n_lat=n_lat, tm=tm),
        out_shape=jax.ShapeDtypeStruct((m, n), F32),
        grid=(m // tm, n // tn),
        in_specs=in_specs,
        out_specs=pl.BlockSpec((tm, tn), lambda i, j: (i, j)),
        compiler_params=_cparams(2),
        name="matmul_residual",
    )(*a_list, w, res, gate2)


NA_HEADS_PER_STEP = 4


def _na_window_start(t, rows_per_tile, win_rows, n_rows):
    return jnp.clip(t * rows_per_tile - NA_WIN_ROWS // 2, 0, n_rows - win_rows)


def _na_kernel(pat_ref, q_ref, k_ref, v_ref, kc_ref, vc_ref, bias_ref, o_ref, *,
               rows_per_tile, win_rows, n_rows):
    del pat_ref
    t = pl.program_id(1)
    scale = HEAD_DIM ** -0.5
    start = pl.multiple_of(_na_window_start(t, rows_per_tile, win_rows, n_rows) * GRID_W, GRID_W)
    is_ctx_tile = t == n_rows // rows_per_tile
    for hh in range(NA_HEADS_PER_STEP):
        lanes = slice(hh * HEAD_DIM, (hh + 1) * HEAD_DIM)
        q = q_ref[:, lanes]
        k_loc = k_ref[pl.ds(start, win_rows * GRID_W), lanes]
        v_loc = v_ref[pl.ds(start, win_rows * GRID_W), lanes]
        bias = jnp.where(is_ctx_tile, NEG_INF, bias_ref[hh])
        s_loc = _dot_nt(q, k_loc) * scale + bias
        s_ctx = _dot_nt(q, kc_ref[:, lanes]) * scale
        m = jnp.maximum(jnp.max(s_loc, axis=-1, keepdims=True), jnp.max(s_ctx, axis=-1, keepdims=True))
        p_loc = jnp.exp(s_loc - m)
        p_ctx = jnp.exp(s_ctx - m)
        denom = jnp.sum(p_loc, axis=-1, keepdims=True) + jnp.sum(p_ctx, axis=-1, keepdims=True)
        o = _dot(p_loc.astype(BF16), v_loc) + _dot(p_ctx.astype(BF16), vc_ref[:, lanes])
        o_ref[:, lanes] = (o / denom).astype(o_ref.dtype)


def _na_tile_patterns(n_rows, rows_per_tile):
    win_rows = rows_per_tile + NA_WIN_ROWS - 1
    keys, pat_of_tile, rep_tile = {}, [], []
    for t in range(n_rows // rows_per_tile):
        r0 = t * rows_per_tile
        us = min(max(r0 - NA_WIN_ROWS // 2, 0), n_rows - win_rows)
        rs = tuple(min(max(r0 + i - NA_WIN_ROWS // 2, 0), n_rows - NA_WIN_ROWS) - us
                   for i in range(rows_per_tile))
        key = (r0 - us, rs)
        if key not in keys:
            keys[key] = len(keys)
            rep_tile.append(t)
        pat_of_tile.append(keys[key])
    return np.asarray(pat_of_tile, np.int32), rep_tile, win_rows


def _na_bias_table(rpb, n_rows, rows_per_tile):
    pat_of_tile, rep_tile, win_rows = _na_tile_patterns(n_rows, rows_per_tile)
    heads = rpb.shape[0]
    w = np.arange(GRID_W)
    col_start = np.clip(w - NA_WIN_COLS // 2, 0, GRID_W - NA_WIN_COLS)
    col_valid = (w[None, :] >= col_start[:, None]) & (w[None, :] < col_start[:, None] + NA_WIN_COLS)
    col_off = w[None, :] - w[:, None] + NA_WIN_COLS - 1
    col_sel = (col_off[:, :, None] == np.arange(2 * NA_WIN_COLS - 1)) & col_valid[:, :, None]
    row_sel, valid = [], []
    for t in rep_tile:
        r0 = t * rows_per_tile
        us = min(max(r0 - NA_WIN_ROWS // 2, 0), n_rows - win_rows)
        r = r0 + np.arange(rows_per_tile)
        rs = np.clip(r - NA_WIN_ROWS // 2, 0, n_rows - NA_WIN_ROWS)
        key_row = us + np.arange(win_rows)
        row_valid = (key_row[None, :] >= rs[:, None]) & (key_row[None, :] < rs[:, None] + NA_WIN_ROWS)
        row_off = key_row[None, :] - r[:, None] + NA_WIN_ROWS - 1
        row_sel.append((row_off[:, :, None] == np.arange(2 * NA_WIN_ROWS - 1)) & row_valid[:, :, None])
        valid.append(row_valid[:, None, :, None] & col_valid[None, :, None, :])
    row_sel = jnp.asarray(np.stack(row_sel), F32)
    valid = jnp.asarray(np.stack(valid))
    toeplitz = jnp.einsum('hrc,wvc->hrwv', rpb.astype(F32), jnp.asarray(col_sel, F32),
                          precision=lax.Precision.HIGHEST)
    bias = jnp.einsum('pijr,hrwv->phiwjv', row_sel, toeplitz, precision=lax.Precision.HIGHEST)
    bias = jnp.where(valid[:, None], bias, NEG_INF)
    bias = bias.reshape(len(rep_tile), heads, rows_per_tile * GRID_W, win_rows * GRID_W)
    return bias, pat_of_tile, win_rows


def neighborhood_attention(qkv, rpb, *, n_lat, n_ctx, rows_per_tile=4):
    heads = rpb.shape[0]
    n_rows = n_lat // GRID_W
    bias, pat_of_tile, win_rows = _na_bias_table(rpb, n_rows, rows_per_tile)
    tq = rows_per_tile * GRID_W
    assert n_ctx == tq, "the context rows must fill exactly one query tile"
    pat_of_tile = np.concatenate([pat_of_tile, np.zeros((1,), np.int32)])
    ctx_blk = n_lat // n_ctx
    hps = NA_HEADS_PER_STEP
    groups = heads // hps
    wide = hps * HEAD_DIM
    grid_spec = pltpu.PrefetchScalarGridSpec(
        num_scalar_prefetch=1,
        grid=(groups, n_rows // rows_per_tile + 1),
        in_specs=[
            pl.BlockSpec((tq, wide), lambda g, t, pat: (t, g)),
            pl.BlockSpec((n_lat, wide), lambda g, t, pat: (0, groups + g)),
            pl.BlockSpec((n_lat, wide), lambda g, t, pat: (0, 2 * groups + g)),
            pl.BlockSpec((n_ctx, wide), lambda g, t, pat: (ctx_blk, groups + g)),
            pl.BlockSpec((n_ctx, wide), lambda g, t, pat: (ctx_blk, 2 * groups + g)),
            pl.BlockSpec((None, hps, tq, win_rows * GRID_W), lambda g, t, pat: (pat[t], g, 0, 0)),
        ],
        out_specs=pl.BlockSpec((tq, wide), lambda g, t, pat: (t, g)),
    )
    return pl.pallas_call(
        functools.partial(_na_kernel, rows_per_tile=rows_per_tile, win_rows=win_rows, n_rows=n_rows),
        out_shape=jax.ShapeDtypeStruct((n_lat + n_ctx, heads * HEAD_DIM), BF16),
        grid_spec=grid_spec,
        compiler_params=_cparams(2),
        name="neighborhood_attention",
    )(jnp.asarray(pat_of_tile), qkv, qkv, qkv, qkv, qkv, bias)


CONV_HALO = 16


def _conformer_kernel(ap_ref, gp_ref, a_ref, g_ref, an_ref, gn_ref, cw_ref, cb_ref, lw_ref, lb_ref, o_ref,
                      buf_ref, y_ref, sh_ref, *, tm, first_tiles, last_tiles):
    i = pl.program_id(0)
    ch = a_ref.shape[1]
    has_prev = jnp.logical_not(functools.reduce(jnp.logical_or, [i == t for t in first_tiles]))
    has_next = jnp.logical_not(functools.reduce(jnp.logical_or, [i == t for t in last_tiles]))

    def glu(a, g):
        return a * jax.nn.sigmoid(g)

    buf_ref[0:CONV_HALO, :] = jnp.where(has_prev, glu(ap_ref[...], gp_ref[...]), 0.0)
    buf_ref[CONV_HALO:CONV_HALO + tm, :] = glu(a_ref[...], g_ref[...])
    buf_ref[CONV_HALO + tm:, :] = jnp.where(has_next, glu(an_ref[...], gn_ref[...]), 0.0)

    base = CONV_HALO - CONV_WIDTH // 2

    n_sh = sh_ref.shape[1]

    def chunk(c, carry):
        col = pl.multiple_of(c * LANES, LANES)
        for phase in range(SUBLANES):
            sh_ref[phase] = buf_ref[phase:phase + n_sh, pl.ds(col, LANES)]
        acc = jnp.zeros((tm, LANES), F32) + cb_ref[:, pl.ds(col, LANES)]
        for tap in range(CONV_WIDTH):
            row, phase = divmod(base + tap, SUBLANES)
            row *= SUBLANES
            acc = acc + sh_ref[phase, row:row + tm, :] * cw_ref[tap:tap + 1, pl.ds(col, LANES)]
        y_ref[:, pl.ds(col, LANES)] = acc
        return carry

    lax.fori_loop(0, ch // LANES, chunk, 0)

    y = y_ref[...]
    mu = jnp.mean(y, axis=-1, keepdims=True)
    yc = y - mu
    z = yc * lax.rsqrt(jnp.mean(yc * yc, axis=-1, keepdims=True) + EPS)
    z = z * lw_ref[...] + lb_ref[...]
    o_ref[...] = (z * jax.nn.sigmoid(z)).astype(o_ref.dtype)


def conformer_conv(ag, cv_w, cv_b, ln_w, ln_b, *, n_lat, tm=256):
    m, two_c = ag.shape
    ch = two_c // 2
    n_tiles = m // tm
    hb = tm // CONV_HALO
    n_hblk = m // CONV_HALO
    first_tiles = (0, n_lat // tm)
    last_tiles = (n_lat // tm - 1, n_tiles - 1)
    prev_map = lambda c: (lambda i: (jnp.maximum(i * hb - 1, 0), c))
    next_map = lambda c: (lambda i: (jnp.minimum((i + 1) * hb, n_hblk - 1), c))
    cur_map = lambda c: (lambda i: (i, c))
    vec = lambda: pl.BlockSpec((1, ch), lambda i: (0, 0))
    return pl.pallas_call(
        functools.partial(_conformer_kernel, tm=tm, first_tiles=first_tiles, last_tiles=last_tiles),
        out_shape=jax.ShapeDtypeStruct((m, ch), BF16),
        grid=(n_tiles,),
        in_specs=[pl.BlockSpec((CONV_HALO, ch), prev_map(0)), pl.BlockSpec((CONV_HALO, ch), prev_map(1)),
                  pl.BlockSpec((tm, ch), cur_map(0)), pl.BlockSpec((tm, ch), cur_map(1)),
                  pl.BlockSpec((CONV_HALO, ch), next_map(0)), pl.BlockSpec((CONV_HALO, ch), next_map(1)),
                  pl.BlockSpec((CONV_WIDTH, ch), lambda i: (0, 0)), vec(), vec(), vec()],
        out_specs=pl.BlockSpec((tm, ch), lambda i: (i, 0)),
        scratch_shapes=[pltpu.VMEM((tm + 2 * CONV_HALO, ch), F32), pltpu.VMEM((tm, ch), F32),
                        pltpu.VMEM((SUBLANES, tm + 2 * CONV_HALO - SUBLANES, LANES), F32)],
        compiler_params=_cparams(1),
        name="conformer_conv",
    )(ag, ag, ag, ag, ag, ag, cv_w, cv_b.reshape(1, ch), ln_w.reshape(1, ch), ln_b.reshape(1, ch))


DIFF_Q_TILE = 256
DIFF_KV_CHUNK = 1536
DIFF_KV_LAST_CHUNK = 768
LOG2E = 1.4426950408889634
DIFF_SCORE_SCALE = HEAD_DIM ** -0.5 * LOG2E


def _diff_attn_kernel(lam_ref, q_ref, k_ref, vt_ref, sw_ref, o_ref, acc1_ref, acc2_ref, *,
                      chunks, lambda_init):
    n_chunks = len(chunks)
    q = (q_ref[:, :HEAD_DIM], q_ref[:, HEAD_DIM:])
    acc_refs = (acc1_ref, acc2_ref)
    tq = q_ref.shape[0]
    m = [jnp.full((1, tq), NEG_INF, F32)] * 2
    l = [jnp.zeros((1, tq), F32)] * 2

    def scores(ci):
        row, rows = chunks[ci]
        kc = k_ref[row:row + rows, :]
        return [_dot_nt(kc[:, b * HEAD_DIM:(b + 1) * HEAD_DIM], q[b]) for b in range(2)]

    def softmax_step(s, b):
        m_new = jnp.maximum(m[b], jnp.max(s, axis=0, keepdims=True))
        alpha = jnp.exp2(m[b] - m_new)
        p = jnp.exp2(s - m_new)
        l[b] = alpha * l[b] + jnp.sum(p, axis=0, keepdims=True)
        m[b] = m_new
        return p.astype(BF16), alpha

    def values(ci, p, alpha, b):
        row, rows = chunks[ci]
        vtc = vt_ref[:, row:row + rows]
        pv = _dot(vtc, p)
        acc_refs[b][...] = pv if ci == 0 else alpha * acc_refs[b][...] + pv

    s_of, p_of = {}, {}
    for t in range(n_chunks + 2):
        if t < n_chunks:
            s_of[t] = scores(t)
        if 0 <= t - 1 < n_chunks:
            p_of[t - 1] = [softmax_step(s, b) for b, s in enumerate(s_of.pop(t - 1))]
        if 0 <= t - 2 < n_chunks:
            for b, (p, alpha) in enumerate(p_of.pop(t - 2)):
                values(t - 2, p, alpha, b)

    lp = lam_ref[...]
    lam = (jnp.exp(jnp.sum(lp[0:1] * lp[1:2], axis=-1, keepdims=True))
           - jnp.exp(jnp.sum(lp[2:3] * lp[3:4], axis=-1, keepdims=True)) + lambda_init)
    o = acc1_ref[...] * (1.0 / l[0]) - acc2_ref[...] * (lam / l[1])
    o = o * lax.rsqrt(jnp.mean(o * o, axis=0, keepdims=True) + EPS)
    o = (o * sw_ref[...]) * (1.0 - lambda_init)
    o_ref[...] = o.T.astype(o_ref.dtype)


def _kv_chunks(key_row0, n_keys):
    assert n_keys % MXU_DIM == 0
    sizes = []
    left = n_keys
    while left > DIFF_KV_CHUNK + DIFF_KV_LAST_CHUNK:
        sizes.append(DIFF_KV_CHUNK)
        left -= DIFF_KV_CHUNK
    if left > DIFF_KV_CHUNK:
        sizes.append(left - DIFF_KV_LAST_CHUNK)
        left = DIFF_KV_LAST_CHUNK
    sizes.append(left)
    starts = np.cumsum([0] + sizes[:-1])
    return tuple((key_row0 + int(s), int(n)) for s, n in zip(starts, sizes))


def diff_attention(qk, vt, lam_p, subln_w, *, heads, q_row0, n_q, key_row0, n_keys, lambda_init, tq):
    m = qk.shape[0]
    hd2 = 2 * HEAD_DIM
    q_blk0 = q_row0 // tq
    scratch = [pltpu.VMEM((hd2, tq), F32), pltpu.VMEM((hd2, tq), F32)]
    return pl.pallas_call(
        functools.partial(_diff_attn_kernel, chunks=_kv_chunks(key_row0, n_keys), lambda_init=lambda_init),
        out_shape=jax.ShapeDtypeStruct((n_q, heads * hd2), BF16),
        grid=(heads, n_q // tq),
        in_specs=[pl.BlockSpec((4, HEAD_DIM), lambda h, i: (0, 0)),
                  pl.BlockSpec((tq, hd2), lambda h, i: (q_blk0 + i, h)),
                  pl.BlockSpec((m, hd2), lambda h, i: (0, heads + h)),
                  pl.BlockSpec((hd2, m), lambda h, i: (h, 0)),
                  pl.BlockSpec((hd2, 1), lambda h, i: (0, 0))],
        out_specs=pl.BlockSpec((tq, hd2), lambda h, i: (i, h)),
        scratch_shapes=scratch,
        compiler_params=_cparams(2),
        name="diff_attention",
    )(lam_p.astype(F32), qk, qk, vt, subln_w.reshape(hd2, 1))


FFN_HALO = 16


def _ffn_up_kernel(ap_ref, a_ref, an_ref, wg_ref, wv_ref, cwg_ref, cbg_ref, cwv_ref, cbv_ref, o_ref,
                   aext_ref, *u_refs, tm, seg_starts, seg_ends):
    @pl.when(pl.program_id(1) == 0)
    def _():
        aext_ref[0:FFN_HALO, :] = ap_ref[...]
        aext_ref[FFN_HALO:FFN_HALO + tm, :] = a_ref[...]
        aext_ref[FFN_HALO + tm:, :] = an_ref[...]

    rid = pl.program_id(0) * tm + lax.broadcasted_iota(jnp.int32, (tm, 1), 0)
    has_prev = functools.reduce(jnp.logical_and, [rid != r for r in seg_starts])
    has_next = functools.reduce(jnp.logical_and, [rid != r for r in seg_ends])

    def conv(u_ref, w, b):
        prev = jnp.where(has_prev, u_ref[FFN_HALO - 1:FFN_HALO - 1 + tm, :], 0.0)
        nxt = jnp.where(has_next, u_ref[FFN_HALO + 1:FFN_HALO + 1 + tm, :], 0.0)
        return b + prev * w[0:1, :] + u_ref[FFN_HALO:FFN_HALO + tm, :] * w[1:2, :] + nxt * w[2:3, :]

    a_ext = aext_ref[...]
    n_sub = len(u_refs) // 2
    sub = o_ref.shape[1] // n_sub
    for s in range(n_sub):
        cols = slice(s * sub, (s + 1) * sub)
        ug_ref, uv_ref = u_refs[2 * s], u_refs[2 * s + 1]
        ug_ref[...] = _dot(a_ext, wg_ref[:, cols])
        uv_ref[...] = _dot(a_ext, wv_ref[:, cols])
        gate = conv(ug_ref, cwg_ref[:, cols], cbg_ref[:, cols])
        val = conv(uv_ref, cwv_ref[:, cols], cbv_ref[:, cols])
        o_ref[:, cols] = (gate * jax.nn.sigmoid(gate) * val).astype(o_ref.dtype)


def ffn_up_conv_gate(a, w_up, layer, conv_w, conv_b, *, n_lat):
    m, k = a.shape
    two_f = w_up.shape[2]
    f = two_f // 2
    tm, tn = _pick_tiles(m, f, lambda tn: 2 * 2 * k * tn * 2,
                         lambda tn: 2 * k * 2 + k * 2 + 2 * tn * 4 + 2 * tn * 2, col_tiles=COL_TILES[:1])
    hb = tm // FFN_HALO
    n_hblk = m // FFN_HALO
    voff = f // tn
    n_sub = tn // MXU_DIM
    b2 = conv_b.reshape(1, two_f)
    u_scratch = [pltpu.VMEM((tm + 2 * FFN_HALO, tn // n_sub), F32) for _ in range(2 * n_sub)]
    return pl.pallas_call(
        functools.partial(_ffn_up_kernel, tm=tm, seg_starts=(0, n_lat), seg_ends=(n_lat - 1, m - 1)),
        out_shape=jax.ShapeDtypeStruct((m, f), BF16),
        grid=(m // tm, f // tn),
        in_specs=[pl.BlockSpec((FFN_HALO, k), lambda i, j: (jnp.maximum(i * hb - 1, 0), 0)),
                  pl.BlockSpec((tm, k), lambda i, j: (i, 0)),
                  pl.BlockSpec((FFN_HALO, k), lambda i, j: (jnp.minimum((i + 1) * hb, n_hblk - 1), 0)),
                  pl.BlockSpec((None, k, tn), lambda i, j: (layer, 0, j)),
                  pl.BlockSpec((None, k, tn), lambda i, j: (layer, 0, j + voff)),
                  pl.BlockSpec((FFN_CONV_WIDTH, tn), lambda i, j: (0, j)),
                  pl.BlockSpec((1, tn), lambda i, j: (0, j)),
                  pl.BlockSpec((FFN_CONV_WIDTH, tn), lambda i, j: (0, j + voff)),
                  pl.BlockSpec((1, tn), lambda i, j: (0, j + voff))],
        out_specs=pl.BlockSpec((tm, tn), lambda i, j: (i, j)),
        scratch_shapes=[pltpu.VMEM((tm + 2 * FFN_HALO, k), BF16)] + u_scratch,
        compiler_params=_cparams(2),
        name="ffn_up_conv_gate",
    )(a, a, a, w_up, w_up, conv_w, b2, conv_w, b2)


def _rope_tables(n_lat, n_ctx):
    t = jnp.arange(n_lat)
    row = (t // GRID_W).astype(F32)
    col = (t % GRID_W).astype(F32)
    per_axis = HEAD_DIM // 2
    inv = ROPE_BASE ** (-jnp.arange(0, per_axis, 2, dtype=F32) / per_axis)
    ang_r, ang_c = row[:, None] * inv, col[:, None] * inv
    cr, sr, cc, sc = jnp.cos(ang_r), jnp.sin(ang_r), jnp.cos(ang_c), jnp.sin(ang_c)
    zero = jnp.zeros_like(sr)
    cos = jnp.concatenate([cr, cr, cc, cc], axis=1)
    sin_a = jnp.concatenate([-sr, zero, -sc, zero], axis=1)
    sin_b = jnp.concatenate([zero, sr, zero, sc], axis=1)
    pad = lambda x, v: jnp.concatenate([x, jnp.full((n_ctx, HEAD_DIM), v, F32)], axis=0)
    return pad(cos, 1.0), pad(sin_a, 0.0), pad(sin_b, 0.0)


def kernel(x, c, ctx, c_ctx, mod_w, mod_b, norm1_w, norm2_w, na_w_in, na_w_out, na_rpb, cv_w, cv_b, cv_ln_w,
           cv_ln_b, diff_w_in, diff_w_out, diff_lambda, diff_subln_w, ffn_w_up, ffn_conv_w, ffn_conv_b,
           ffn_w_down, final_norm_w):
    assert x.shape[0] == 1 and ctx.shape[0] == 1
    n_lat, d = x.shape[1], x.shape[2]
    n_ctx = ctx.shape[1]
    depth = mod_w.shape[0]
    m = n_lat + n_ctx
    na_width = na_rpb.shape[1] * HEAD_DIM
    diff_heads = diff_w_in.shape[2] // (3 * 2 * HEAD_DIM)
    diff_width = diff_heads * 2 * HEAD_DIM

    h = jnp.concatenate([x[0], ctx[0]], axis=0)
    mod = adaln_modulation(jnp.stack([c[0], c_ctx], axis=1), mod_w, mod_b)
    mod = mod.reshape(depth, 2, 6, d)
    cos, sin_a, sin_b = _rope_tables(n_lat, n_ctx)
    ffn_w_up_bf16 = ffn_w_up.astype(BF16)

    for l in range(depth):
        i = l // 2
        shift1, scale1, gate1, shift2, scale2, gate2 = (mod[l, :, j, :] for j in range(6))
        a = norm_modulate(h, norm1_w[l], shift1, scale1, n_lat=n_lat, out_dtype=BF16)
        if l % 2 == 0:
            qkv = matmul(a, na_w_in, i, col0=0, n_cols=3 * na_width, out_dtype=BF16)
            ag = matmul(a, na_w_in, i, col0=3 * na_width, n_cols=na_w_in.shape[2] - 3 * na_width, out_dtype=F32)
            o_na = neighborhood_attention(qkv, na_rpb[i], n_lat=n_lat, n_ctx=n_ctx)
            o_cv = conformer_conv(ag, cv_w[i], cv_b[i], cv_ln_w[i], cv_ln_b[i], n_lat=n_lat)
            h = matmul_residual([o_na, o_cv], na_w_out, i, h, gate1, n_lat=n_lat)
        else:
            lambda_init = 0.8 - 0.6 * math.exp(-0.3 * l)
            qk = matmul_rope(a, diff_w_in, i, cos, sin_a, sin_b, n_cols=2 * diff_width, scaled_cols=diff_width,
                             scale=DIFF_SCORE_SCALE, out_dtype=BF16)
            vt = matmul_transposed_out(a, diff_w_in, i, col0=2 * diff_width, n_cols=diff_width, out_dtype=BF16)
            o_lat = diff_attention(qk, vt, diff_lambda[i], diff_subln_w[i], heads=diff_heads, q_row0=0, n_q=n_lat,
                                   key_row0=0, n_keys=m, lambda_init=lambda_init, tq=min(DIFF_Q_TILE, n_lat))
            o_ctx = diff_attention(qk, vt, diff_lambda[i], diff_subln_w[i], heads=diff_heads, q_row0=n_lat,
                                   n_q=n_ctx, key_row0=n_lat, n_keys=n_ctx, lambda_init=lambda_init, tq=n_ctx)
            o = jnp.concatenate([o_lat, o_ctx], axis=0)
            h = matmul_residual([o], diff_w_out, i, h, gate1, n_lat=n_lat)
        f = norm_modulate(h, norm2_w[l], shift2, scale2, n_lat=n_lat, out_dtype=BF16)
        act = ffn_up_conv_gate(f, ffn_w_up_bf16, l, ffn_conv_w[l], ffn_conv_b[l], n_lat=n_lat)
        h = matmul_residual([act], ffn_w_down, l, h, gate2, n_lat=n_lat)

    zeros2 = jnp.zeros((2, d), F32)
    out = norm_modulate(h, final_norm_w, zeros2, zeros2, n_lat=n_lat, out_dtype=F32, n_rows=n_lat)
    return out[None]
```
